```python
import math
import jax, jax.numpy as jnp
from jax import lax
import numpy as np

D_MODEL = 1024
BATCH = 8
SEQ = 4096
DEPTH = 4

MLA_HEADS = 8
MLA_NOPE = 64
MLA_ROPE = 32
MLA_V = 64
MLA_Q_RANK = 384
MLA_KV_RANK = 256
DIFF_HEADS = 4
DIFF_HD = 64
DIFF_VD = 2 * DIFF_HD
CONV_CH = 512
CONV_K = 31
D_FF = 2816
FFN_CONV_K = 3
ROPE_THETA = 10000.0
Q_BLOCK = 128
NORM_EPS = 1e-6
N_BRANCH = 3
N_MOD = 6

DIFF_Q_COLS = DIFF_HEADS * 2 * DIFF_HD
DIFF_V_COLS = DIFF_HEADS * DIFF_VD
CONV_COLS = 2 * CONV_CH
GATE_COLS = N_BRANCH * D_MODEL
IN_SIZES = (MLA_Q_RANK, MLA_KV_RANK, MLA_ROPE, DIFF_Q_COLS, DIFF_Q_COLS, DIFF_V_COLS, CONV_COLS, GATE_COLS)
IN_COLS = sum(IN_SIZES)

kernel_name = 'hybrid_mla_diffattn_conformer_convffn_adaln'


def rms_norm(x, g):
    xf = x.astype(jnp.float32)
    y = xf * lax.rsqrt(jnp.mean(xf * xf, axis=-1, keepdims=True) + NORM_EPS)
    return (y * g.astype(jnp.float32)).astype(x.dtype)


def layer_norm(x, g, b):
    xf = x.astype(jnp.float32)
    mu = jnp.mean(xf, axis=-1, keepdims=True)
    xc = xf - mu
    var = jnp.mean(xc * xc, axis=-1, keepdims=True)
    return (xc * lax.rsqrt(var + NORM_EPS) * g.astype(jnp.float32) + b.astype(jnp.float32)).astype(x.dtype)


def rope_tables(positions, dim):
    inv = 1.0 / (ROPE_THETA ** (jnp.arange(0, dim, 2, dtype=jnp.float32) / dim))
    ang = positions.astype(jnp.float32)[..., None] * inv
    return jnp.cos(ang), jnp.sin(ang)


def apply_rope(x, cos, sin):
    shape = (cos.shape[0],) + (1,) * (x.ndim - 3) + cos.shape[1:]
    cs = cos.reshape(shape)
    sn = sin.reshape(shape)
    xf = x.astype(jnp.float32)
    x1, x2 = jnp.split(xf, 2, axis=-1)
    return jnp.concatenate([x1 * cs - x2 * sn, x2 * cs + x1 * sn], axis=-1).astype(x.dtype)


def causal_dwconv(x, w, b):
    k = w.shape[0]
    y = lax.conv_general_dilated(x, w[:, None, :].astype(x.dtype), window_strides=(1,), padding=[(k - 1, 0)],
                                 dimension_numbers=('NWC', 'WIO', 'NWC'), feature_group_count=x.shape[-1])
    return y + b.astype(x.dtype)


def causal_block_attention(q, k, v, scale, combine):
    bsz, g, m, s, dk = q.shape
    nb = s // Q_BLOCK
    qb = q.reshape(bsz, g, m, nb, Q_BLOCK, dk).transpose(3, 0, 1, 2, 4, 5)
    kpos = jnp.arange(s)

    def body(args):
        qi, i = args
        sc = jnp.einsum('bgmtd,bgmsd->bgmts', qi, k).astype(jnp.float32) * scale
        qpos = i * Q_BLOCK + jnp.arange(Q_BLOCK)
        mask = kpos[None, :] <= qpos[:, None]
        sc = jnp.where(mask, sc, jnp.finfo(jnp.float32).min)
        p = jax.nn.softmax(sc, axis=-1)
        w = combine(p).astype(v.dtype)
        return jnp.einsum('bgts,bgsd->bgtd', w, v)

    out = lax.map(body, (qb, jnp.arange(nb)))
    return out.transpose(1, 2, 0, 3, 4).reshape(bsz, g, s, v.shape[-1])


def token_mixer(h, cos_r, sin_r, cos_d, sin_d, lambda_init, w_in, q_norm_g, w_uq, kv_norm_g, w_ukv, w_out_mla,
                diff_lambda, diff_subln_g, w_out_diff, conv_dw_w, conv_dw_b, conv_ln_g, conv_ln_b, w_out_conv, w_o):
    bsz, s, _ = h.shape
    proj = h @ w_in
    q_lat, kv_lat, k_rope, dq, dk, dv, conv_in, gate_in = jnp.split(
        proj, np.cumsum(IN_SIZES)[:-1].tolist(), axis=-1)

    q = rms_norm(q_lat, q_norm_g) @ w_uq
    q = q.reshape(bsz, s, MLA_HEADS, MLA_NOPE + MLA_ROPE).transpose(0, 2, 1, 3)
    q_nope, q_pe = q[..., :MLA_NOPE], q[..., MLA_NOPE:]
    q_pe = apply_rope(q_pe, cos_r, sin_r)
    kv = rms_norm(kv_lat, kv_norm_g) @ w_ukv
    kv = kv.reshape(bsz, s, MLA_HEADS, MLA_NOPE + MLA_V).transpose(0, 2, 1, 3)
    k_nope, v_a = kv[..., :MLA_NOPE], kv[..., MLA_NOPE:]
    k_pe = apply_rope(k_rope[:, None], cos_r, sin_r)
    qa = jnp.concatenate([q_nope, q_pe], axis=-1)[:, :, None]
    ka = jnp.concatenate([k_nope, jnp.broadcast_to(k_pe, (bsz, MLA_HEADS, s, MLA_ROPE))], axis=-1)[:, :, None]
    o_a = causal_block_attention(qa, ka, v_a, 1.0 / math.sqrt(MLA_NOPE + MLA_ROPE), lambda p: p[:, :, 0])
    y_a = o_a.transpose(0, 2, 1, 3).reshape(bsz, s, MLA_HEADS * MLA_V) @ w_out_mla

    qd = apply_rope(dq.reshape(bsz, s, DIFF_HEADS, 2, DIFF_HD).transpose(0, 2, 3, 1, 4), cos_d, sin_d)
    kd = apply_rope(dk.reshape(bsz, s, DIFF_HEADS, 2, DIFF_HD).transpose(0, 2, 3, 1, 4), cos_d, sin_d)
    vd = dv.reshape(bsz, s, DIFF_HEADS, DIFF_VD).transpose(0, 2, 1, 3)
    dl = diff_lambda.astype(jnp.float32)
    lam = jnp.exp(jnp.sum(dl[0] * dl[1])) - jnp.exp(jnp.sum(dl[2] * dl[3])) + lambda_init
    o_b = causal_block_attention(qd, kd, vd, 1.0 / math.sqrt(DIFF_HD),
                                 lambda p: p[:, :, 0] - lam * p[:, :, 1])
    o_b = rms_norm(o_b, diff_subln_g) * (1.0 - lambda_init)
    y_b = o_b.transpose(0, 2, 1, 3).reshape(bsz, s, DIFF_HEADS * DIFF_VD) @ w_out_diff

    a, g = jnp.split(conv_in, 2, axis=-1)
    u = a * jax.nn.sigmoid(g)
    u = causal_dwconv(u, conv_dw_w, conv_dw_b)
    u = jax.nn.silu(layer_norm(u, conv_ln_g, conv_ln_b))
    y_c = u @ w_out_conv

    gates = jax.nn.sigmoid(gate_in).reshape(bsz, s, N_BRANCH, D_MODEL)
    merged = gates[:, :, 0] * y_a + gates[:, :, 1] * y_b + gates[:, :, 2] * y_c
    return merged @ w_o


def conv_gated_mlp(h, w_up, ffn_conv_w, ffn_conv_b, w_down):
    u = causal_dwconv(h @ w_up, ffn_conv_w, ffn_conv_b)
    a, b = jnp.split(u, 2, axis=-1)
    return (jax.nn.silu(a) * b) @ w_down


def setup_inputs(seed: int = 0) -> dict:
    key = jax.random.key(seed)
    ks = jax.random.split(key, 32)
    f32 = jnp.float32
    L, D = DEPTH, D_MODEL

    def dense(k, shape, fan_in, gain=1.0):
        return jax.random.normal(k, shape, f32) * (gain * fan_in ** -0.5)

    def gain_vec(k, shape):
        return 1.0 + 0.05 * jax.random.normal(k, shape, f32)

    def bias(k, shape):
        return 0.01 * jax.random.normal(k, shape, f32)

    x = jax.random.normal(ks[0], (BATCH, SEQ, D), f32)
    c = jax.random.normal(ks[1], (BATCH, D), f32)
    positions = (jnp.arange(SEQ, dtype=jnp.int32)[None, :]
                 + jax.random.randint(ks[2], (BATCH, 1), 0, 1024, dtype=jnp.int32))
    return {
        'x': x,
        'c': c,
        'positions': positions,
        'ada_w': dense(ks[3], (L, D, N_MOD * D), D, 0.5),
        'ada_b': bias(ks[4], (L, N_MOD * D)),
        'norm1_g': gain_vec(ks[5], (L, D)),
        'w_in': dense(ks[6], (L, D, IN_COLS), D),
        'q_norm_g': gain_vec(ks[7], (L, MLA_Q_RANK)),
        'w_uq': dense(ks[8], (L, MLA_Q_RANK, MLA_HEADS * (MLA_NOPE + MLA_ROPE)), MLA_Q_RANK),
        'kv_norm_g': gain_vec(ks[9], (L, MLA_KV_RANK)),
        'w_ukv': dense(ks[10], (L, MLA_KV_RANK, MLA_HEADS * (MLA_NOPE + MLA_V)), MLA_KV_RANK),
        'w_out_mla': dense(ks[11], (L, MLA_HEADS * MLA_V, D), MLA_HEADS * MLA_V),
        'diff_lambda': 0.1 * jax.random.normal(ks[12], (L, 4, DIFF_HD), f32),
        'diff_subln_g': gain_vec(ks[13], (L, DIFF_VD)),
        'w_out_diff': dense(ks[14], (L, DIFF_HEADS * DIFF_VD, D), DIFF_HEADS * DIFF_VD),
        'conv_dw_w': dense(ks[15], (L, CONV_K, CONV_CH), CONV_K),
        'conv_dw_b': bias(ks[16], (L, CONV_CH)),
        'conv_ln_g': gain_vec(ks[17], (L, CONV_CH)),
        'conv_ln_b': bias(ks[18], (L, CONV_CH)),
        'w_out_conv': dense(ks[19], (L, CONV_CH, D), CONV_CH),
        'w_o': dense(ks[20], (L, D, D), D),
        'norm2_g': gain_vec(ks[21], (L, D)),
        'w_up': dense(ks[22], (L, D, 2 * D_FF), D),
        'ffn_conv_w': dense(ks[23], (L, FFN_CONV_K, 2 * D_FF), FFN_CONV_K),
        'ffn_conv_b': bias(ks[24], (L, 2 * D_FF)),
        'w_down': dense(ks[25], (L, D_FF, D), D_FF),
        'final_g': gain_vec(ks[26], (D,)),
    }


def reference(x, c, positions, ada_w, ada_b, norm1_g, w_in, q_norm_g, w_uq, kv_norm_g, w_ukv, w_out_mla,
              diff_lambda, diff_subln_g, w_out_diff, conv_dw_w, conv_dw_b, conv_ln_g, conv_ln_b, w_out_conv, w_o,
              norm2_g, w_up, ffn_conv_w, ffn_conv_b, w_down, final_g):
    cos_r, sin_r = rope_tables(positions, MLA_ROPE)
    cos_d, sin_d = rope_tables(positions, DIFF_HD)
    cos_r, sin_r, cos_d, sin_d = (t.astype(x.dtype) for t in (cos_r, sin_r, cos_d, sin_d))
    c_act = jax.nn.silu(c)
    for l in range(DEPTH):
        lambda_init = 0.8 - 0.6 * math.exp(-0.3 * l)
        mod = (c_act @ ada_w[l] + ada_b[l])[:, None, :]
        sh1, sc1, gt1, sh2, sc2, gt2 = jnp.split(mod, N_MOD, axis=-1)
        h = rms_norm(x, norm1_g[l]) * (1.0 + sc1) + sh1
        x = x + gt1 * token_mixer(h, cos_r, sin_r, cos_d, sin_d, lambda_init, w_in[l], q_norm_g[l], w_uq[l],
                                  kv_norm_g[l], w_ukv[l], w_out_mla[l], diff_lambda[l], diff_subln_g[l],
                                  w_out_diff[l], conv_dw_w[l], conv_dw_b[l], conv_ln_g[l], conv_ln_b[l],
                                  w_out_conv[l], w_o[l])
        h = rms_norm(x, norm2_g[l]) * (1.0 + sc2) + sh2
        x = x + gt2 * conv_gated_mlp(h, w_up[l], ffn_conv_w[l], ffn_conv_b[l], w_down[l])
    return rms_norm(x, final_g)
```

```python
import functools
import math

import jax
import jax.numpy as jnp
from jax import lax
from jax.experimental import pallas as pl
from jax.experimental.pallas import tpu as pltpu

F32 = jnp.float32
BF16 = jnp.bfloat16

D_MODEL = 1024
MLA_HEADS = 8
MLA_NOPE = 64
MLA_ROPE = 32
MLA_V = 64
MLA_Q_RANK = 384
MLA_KV_RANK = 256
DIFF_HEADS = 4
DIFF_HD = 64
DIFF_VD = 2 * DIFF_HD
CONV_CH = 512
CONV_K = 31
D_FF = 2816
FFN_CONV_K = 3
ROPE_THETA = 10000.0
NORM_EPS = 1e-6
N_MOD = 6

LANES = 128
CONV_HALO = 32
FFN_HALO = 8
VMEM_LIMIT = 56 * 1024 * 1024
NEG_BIG = -1e30

DIFF_COLS = DIFF_HEADS * 2 * DIFF_HD
MLA_PAD_COLS = MLA_HEADS * LANES


def _sigmoid(x):
    return 1.0 / (1.0 + jnp.exp(-x))


def _rms(x, g):
    return x * lax.rsqrt(jnp.mean(x * x, axis=-1, keepdims=True) + NORM_EPS) * g


def _rot(x, tab_ref):
    c = tab_ref[:, 0:LANES]
    s_up = tab_ref[:, LANES:2 * LANES]
    s_dn = tab_ref[:, 2 * LANES:3 * LANES]
    return x, c, s_up, s_dn


def _apply_rot(x, c, s_up, s_dn, half):
    return x * c + pltpu.roll(x, LANES - half, 1) * s_up + pltpu.roll(x, half, 1) * s_dn


def _const_spec(shape):
    nd = len(shape)
    return pl.BlockSpec(shape, lambda *_: (0,) * nd, pipeline_mode=pl.Buffered(1))


def _mod_kernel(c_ref, w_ref, b_ref, o_ref):
    c = c_ref[...]
    c_act = (c * _sigmoid(c)).astype(BF16)
    o_ref[...] = jnp.dot(c_act, w_ref[...].astype(BF16), preferred_element_type=F32) + b_ref[...]


def _modulation(c, ada_w, ada_b):
    depth, d, _ = ada_w.shape
    bsz = c.shape[0]
    out = pl.pallas_call(
        _mod_kernel,
        grid=(depth, N_MOD),
        in_specs=[
            pl.BlockSpec((bsz, d), lambda l, j: (0, 0)),
            pl.BlockSpec((None, d, d), lambda l, j: (l, 0, j)),
            pl.BlockSpec((None, 1, d), lambda l, j: (l * N_MOD + j, 0, 0)),
        ],
        out_specs=pl.BlockSpec((None, None, bsz, d), lambda l, j: (l, j, 0, 0)),
        out_shape=jax.ShapeDtypeStruct((depth, N_MOD, bsz, d), F32),
        compiler_params=pltpu.CompilerParams(dimension_semantics=("arbitrary", "arbitrary"),
                                             vmem_limit_bytes=VMEM_LIMIT),
        name="adaln_mod",
    )(c, ada_w, ada_b.reshape(depth * N_MOD, 1, d))
    return out.transpose(0, 2, 1, 3)


def _pre_kernel(x_ref, mod_ref, g1_ref, wq_ref, wkv_ref, wd_ref, wc_ref, wuq_ref, wukv_ref, qg_ref, kvg_ref,
                tq_ref, tk_ref, td_ref, q_out, k_out, v_out, dq_out, dk_out, dv_out, u_out):
    x = x_ref[...]
    h = _rms(x, g1_ref[...]) * (1.0 + mod_ref[1:2, :]) + mod_ref[0:1, :]
    hb = h.astype(BF16)

    q_lat = jnp.dot(hb, wq_ref[...], preferred_element_type=F32)
    qn = _rms(q_lat, qg_ref[...]).astype(BF16)
    q = jnp.dot(qn, wuq_ref[...], preferred_element_type=F32)
    half = MLA_ROPE // 2
    cq, su, sd = tq_ref[:, 0:LANES], tq_ref[:, LANES:2 * LANES], tq_ref[:, 2 * LANES:3 * LANES]
    for hd in range(MLA_HEADS):
        blk = slice(hd * LANES, (hd + 1) * LANES)
        q_out[:, blk] = _apply_rot(q[:, blk], cq, su, sd, half).astype(BF16)

    kv_lat = jnp.dot(hb, wkv_ref[...], preferred_element_type=F32)
    kvn = _rms(kv_lat[:, :MLA_KV_RANK], kvg_ref[...]).astype(BF16)
    kv = jnp.dot(kvn, wukv_ref[...], preferred_element_type=F32)
    ck, su, sd = tk_ref[:, 0:LANES], tk_ref[:, LANES:2 * LANES], tk_ref[:, 2 * LANES:3 * LANES]
    k_pe = _apply_rot(kv_lat[:, MLA_KV_RANK:MLA_KV_RANK + LANES], ck, su, sd, half)
    for hd in range(MLA_HEADS):
        blk = slice(hd * LANES, (hd + 1) * LANES)
        k_out[:, blk] = (kv[:, blk] + k_pe).astype(BF16)
    v_out[...] = kv[:, MLA_PAD_COLS:].astype(BF16)

    d = jnp.dot(hb, wd_ref[...], preferred_element_type=F32)
    half = DIFF_HD // 2
    cd, su, sd = td_ref[:, 0:LANES], td_ref[:, LANES:2 * LANES], td_ref[:, 2 * LANES:3 * LANES]
    scale = 1.0 / math.sqrt(DIFF_HD)
    for hd in range(DIFF_HEADS):
        blk = slice(hd * LANES, (hd + 1) * LANES)
        kblk = slice(DIFF_COLS + hd * LANES, DIFF_COLS + (hd + 1) * LANES)
        dq_out[:, blk] = (_apply_rot(d[:, blk], cd, su, sd, half) * scale).astype(BF16)
        dk_out[:, blk] = _apply_rot(d[:, kblk], cd, su, sd, half).astype(BF16)
    dv_out[...] = d[:, 2 * DIFF_COLS:].astype(BF16)

    cv = jnp.dot(hb, wc_ref[...], preferred_element_type=F32)
    u_out[...] = cv[:, :CONV_CH] * _sigmoid(cv[:, CONV_CH:])


def _pre_mixer(x, mod_l, g1, wq, wkv, wd, wc, wuq, wukv, qg, kvg, tab_q, tab_k, tab_d, tm):
    bsz, s, d = x.shape
    row = lambda w: pl.BlockSpec((None, tm, w), lambda b, i: (b, i, 0))
    outs = [(MLA_PAD_COLS, BF16), (MLA_PAD_COLS, BF16), (MLA_HEADS * MLA_V, BF16),
            (DIFF_COLS, BF16), (DIFF_COLS, BF16), (DIFF_COLS, BF16), (CONV_CH, F32)]
    return pl.pallas_call(
        _pre_kernel,
        grid=(bsz, s // tm),
        in_specs=[row(d), pl.BlockSpec((None, N_MOD, d), lambda b, i: (b, 0, 0)), _const_spec(g1.shape),
                  _const_spec(wq.shape), _const_spec(wkv.shape), _const_spec(wd.shape), _const_spec(wc.shape),
                  _const_spec(wuq.shape), _const_spec(wukv.shape), _const_spec(qg.shape), _const_spec(kvg.shape),
                  row(3 * LANES), row(3 * LANES), row(3 * LANES)],
        out_specs=[row(w) for w, _ in outs],
        out_shape=[jax.ShapeDtypeStruct((bsz, s, w), dt) for w, dt in outs],
        compiler_params=pltpu.CompilerParams(dimension_semantics=("parallel", "parallel"),
                                             vmem_limit_bytes=VMEM_LIMIT),
        name="pre_mixer",
    )(x, mod_l, g1, wq, wkv, wd, wc, wuq, wukv, qg, kvg, tab_q, tab_k, tab_d)


def _flash_rows(q, k_ref, v_ref, k_cols, row0, tq, tk):
    n_full = row0 // tk

    def block(kb, carry, masked):
        m, l, acc = carry
        ks = pl.multiple_of(kb * tk, tk)
        k = k_ref[pl.ds(ks, tk), k_cols]
        v = v_ref[pl.ds(ks, tk), :]
        s = lax.dot_general(q, k, (((1,), (1,)), ((), ())), preferred_element_type=F32)
        if masked:
            rows = row0 + lax.broadcasted_iota(jnp.int32, (tq, tk), 0)
            cols = ks + lax.broadcasted_iota(jnp.int32, (tq, tk), 1)
            s = jnp.where(cols <= rows, s, NEG_BIG)
        m_new = jnp.maximum(m, jnp.max(s, axis=-1, keepdims=True))
        alpha = jnp.exp(m - m_new)
        p = jnp.exp(s - m_new)
        l = alpha * l + jnp.sum(p, axis=-1, keepdims=True)
        acc = alpha * acc + jnp.dot(p.astype(BF16), v, preferred_element_type=F32)
        return m_new, l, acc

    init = (jnp.full((tq, 1), NEG_BIG, F32), jnp.zeros((tq, 1), F32), jnp.zeros((tq, v_ref.shape[-1]), F32))
    carry = lax.fori_loop(0, n_full, lambda kb, c: block(kb, c, False), init)
    _, l, acc = block(n_full, carry, True)
    return acc / l


def _mla_attn_kernel(q_ref, k_ref, v_ref, o_ref, *, tq, tk):
    row0 = pl.program_id(2) * tq
    lo = _flash_rows(q_ref[:, 0:LANES], k_ref, v_ref, slice(0, LANES), row0, tq, tk)
    hi = _flash_rows(q_ref[:, LANES:2 * LANES], k_ref, v_ref, slice(LANES, 2 * LANES), row0, tq, tk)
    lane = lax.broadcasted_iota(jnp.int32, (tq, LANES), 1)
    o_ref[...] = jnp.where(lane < MLA_V, lo, hi).astype(BF16)


def _mla_attention(q, k, v, tq, tk):
    bsz, s, _ = q.shape
    return pl.pallas_call(
        functools.partial(_mla_attn_kernel, tq=tq, tk=tk),
        grid=(bsz, MLA_HEADS // 2, s // tq),
        in_specs=[pl.BlockSpec((None, tq, 2 * LANES), lambda b, j, i: (b, i, j)),
                  pl.BlockSpec((None, s, 2 * LANES), lambda b, j, i: (b, 0, j)),
                  pl.BlockSpec((None, s, LANES), lambda b, j, i: (b, 0, j))],
        out_specs=pl.BlockSpec((None, tq, LANES), lambda b, j, i: (b, i, j)),
        out_shape=jax.ShapeDtypeStruct((bsz, s, MLA_HEADS * MLA_V), BF16),
        compiler_params=pltpu.CompilerParams(dimension_semantics=("parallel", "parallel", "parallel"),
                                             vmem_limit_bytes=VMEM_LIMIT),
        name="mla_attention",
    )(q, k, v)


def _diff_attn_kernel(q_ref, k_ref, v_ref, dl_ref, g_ref, o_ref, *, tq, tk, lambda_init):
    row0 = pl.program_id(2) * tq
    q = q_ref[...]
    lane = lax.broadcasted_iota(jnp.int32, (tq, LANES), 1)
    zero = jnp.zeros_like(q)
    o1 = _flash_rows(jnp.where(lane < DIFF_HD, q, zero), k_ref, v_ref, slice(None), row0, tq, tk)
    o2 = _flash_rows(jnp.where(lane < DIFF_HD, zero, q), k_ref, v_ref, slice(None), row0, tq, tk)
    dl = dl_ref[...]
    lam = (jnp.exp(jnp.sum(dl[0:1, :] * dl[1:2, :], keepdims=True))
           - jnp.exp(jnp.sum(dl[2:3, :] * dl[3:4, :], keepdims=True)) + lambda_init)
    o = o1 - lam * o2
    o_ref[...] = (_rms(o, g_ref[...]) * (1.0 - lambda_init)).astype(BF16)


def _diff_attention(q, k, v, dl, g, tq, tk, lambda_init):
    bsz, s, _ = q.shape
    return pl.pallas_call(
        functools.partial(_diff_attn_kernel, tq=tq, tk=tk, lambda_init=lambda_init),
        grid=(bsz, DIFF_HEADS, s // tq),
        in_specs=[pl.BlockSpec((None, tq, LANES), lambda b, j, i: (b, i, j)),
                  pl.BlockSpec((None, s, LANES), lambda b, j, i: (b, 0, j)),
                  pl.BlockSpec((None, s, LANES), lambda b, j, i: (b, 0, j)),
                  pl.BlockSpec(dl.shape, lambda b, j, i: (0, 0)),
                  pl.BlockSpec(g.shape, lambda b, j, i: (0, 0))],
        out_specs=pl.BlockSpec((None, tq, LANES), lambda b, j, i: (b, i, j)),
        out_shape=jax.ShapeDtypeStruct((bsz, s, DIFF_HEADS * DIFF_VD), BF16),
        compiler_params=pltpu.CompilerParams(dimension_semantics=("parallel", "parallel", "parallel"),
                                             vmem_limit_bytes=VMEM_LIMIT),
        name="diff_attention",
    )(q, k, v, dl, g)


def _post_kernel(x_ref, mod_ref, g1_ref, oa_ref, ob_ref, u_ref, wg_ref, woa_ref, wob_ref, woc_ref, wo_ref,
                 cw_ref, cb_ref, lng_ref, lnb_ref, out_ref, ubuf, cbuf, *, tm, chunk):
    i = pl.program_id(1)

    @pl.when(i == 0)
    def _():
        ubuf[0:CONV_HALO, :] = jnp.zeros((CONV_HALO, CONV_CH), F32)

    @pl.when(i > 0)
    def _():
        ubuf[0:CONV_HALO, :] = ubuf[tm:tm + CONV_HALO, :]

    ubuf[CONV_HALO:tm + CONV_HALO, :] = u_ref[...]

    base = CONV_HALO - (CONV_K - 1)
    for c in range(tm // chunk):
        acc = jnp.broadcast_to(cb_ref[...], (chunk, CONV_CH))
        for j in range(CONV_K):
            r = c * chunk + base + j
            acc = acc + cw_ref[j:j + 1, :] * ubuf[r:r + chunk, :]
        cbuf[c * chunk:(c + 1) * chunk, :] = acc
    uc = cbuf[...]
    mu = jnp.mean(uc, axis=-1, keepdims=True)
    xc = uc - mu
    var = jnp.mean(xc * xc, axis=-1, keepdims=True)
    ln = xc * lax.rsqrt(var + NORM_EPS) * lng_ref[...] + lnb_ref[...]
    y_c = jnp.dot((ln * _sigmoid(ln)).astype(BF16), woc_ref[...], preferred_element_type=F32)

    x = x_ref[...]
    h = _rms(x, g1_ref[...]) * (1.0 + mod_ref[1:2, :]) + mod_ref[0:1, :]
    hb = h.astype(BF16)
    d = x.shape[-1]
    merged = _sigmoid(jnp.dot(hb, wg_ref[:, 2 * d:3 * d], preferred_element_type=F32)) * y_c
    y_a = jnp.dot(oa_ref[...], woa_ref[...], preferred_element_type=F32)
    merged = merged + _sigmoid(jnp.dot(hb, wg_ref[:, 0:d], preferred_element_type=F32)) * y_a
    y_b = jnp.dot(ob_ref[...], wob_ref[...], preferred_element_type=F32)
    merged = merged + _sigmoid(jnp.dot(hb, wg_ref[:, d:2 * d], preferred_element_type=F32)) * y_b
    out_ref[...] = x + mod_ref[2:3, :] * jnp.dot(merged.astype(BF16), wo_ref[...], preferred_element_type=F32)


def _post_mixer(x, mod_l, g1, o_a, o_b, u, wg, woa, wob, woc, wo, cw, cb, lng, lnb, tm):
    bsz, s, d = x.shape
    row = lambda w: pl.BlockSpec((None, tm, w), lambda b, i: (b, i, 0))
    consts = (g1,)
    weights = (wg, woa, wob, woc, wo, cw, cb, lng, lnb)
    return pl.pallas_call(
        functools.partial(_post_kernel, tm=tm, chunk=min(32, tm)),
        grid=(bsz, s // tm),
        in_specs=[row(d), pl.BlockSpec((None, N_MOD, d), lambda b, i: (b, 0, 0))]
                 + [_const_spec(a.shape) for a in consts]
                 + [row(o_a.shape[-1]), row(o_b.shape[-1]), row(CONV_CH)]
                 + [_const_spec(a.shape) for a in weights],
        out_specs=row(d),
        out_shape=jax.ShapeDtypeStruct((bsz, s, d), F32),
        scratch_shapes=[pltpu.VMEM((tm + CONV_HALO, CONV_CH), F32), pltpu.VMEM((tm, CONV_CH), F32)],
        compiler_params=pltpu.CompilerParams(dimension_semantics=("arbitrary", "arbitrary"),
                                             vmem_limit_bytes=VMEM_LIMIT),
        name="post_mixer",
    )(x, mod_l, g1, o_a, o_b, u, *weights)


def _ffn_kernel(x_ref, mod_ref, g2_ref, wup_ref, fcw_ref, fcb_ref, wdn_ref, fg_ref, out_ref, ext, carry,
                *, tm, tf, final):
    i = pl.program_id(1)

    @pl.when(i == 0)
    def _():
        carry[...] = jnp.zeros(carry.shape, F32)

    x = x_ref[...]
    h = _rms(x, g2_ref[...]) * (1.0 + mod_ref[4:5, :]) + mod_ref[3:4, :]
    hb = h.astype(BF16)
    acc = jnp.zeros(x.shape, F32)
    for c in range(D_FF // tf):
        halves = []
        for part in range(2):
            cols = slice(part * D_FF + c * tf, part * D_FF + (c + 1) * tf)
            up = jnp.dot(hb, wup_ref[:, cols], preferred_element_type=F32)
            ext[0:FFN_HALO, :] = carry[:, cols]
            ext[FFN_HALO:tm + FFN_HALO, :] = up
            carry[:, cols] = ext[tm:tm + FFN_HALO, :]
            y = fcb_ref[:, cols] + fcw_ref[2:3, cols] * ext[FFN_HALO:tm + FFN_HALO, :]
            y = y + fcw_ref[1:2, cols] * ext[FFN_HALO - 1:tm + FFN_HALO - 1, :]
            y = y + fcw_ref[0:1, cols] * ext[FFN_HALO - 2:tm + FFN_HALO - 2, :]
            halves.append(y)
        a, b = halves
        gated = (a * _sigmoid(a) * b).astype(BF16)
        acc = acc + jnp.dot(gated, wdn_ref[c * tf:(c + 1) * tf, :], preferred_element_type=F32)
    y = x + mod_ref[5:6, :] * acc
    if final:
        y = _rms(y, fg_ref[...])
    out_ref[...] = y


def _ffn(x, mod_l, g2, wup, fcw, fcb, wdn, fg, tm, tf, final):
    bsz, s, d = x.shape
    row = pl.BlockSpec((None, tm, d), lambda b, i: (b, i, 0))
    consts = (g2, wup, fcw, fcb, wdn, fg)
    return pl.pallas_call(
        functools.partial(_ffn_kernel, tm=tm, tf=tf, final=final),
        grid=(bsz, s // tm),
        in_specs=[row, pl.BlockSpec((None, N_MOD, d), lambda b, i: (b, 0, 0))]
                 + [_const_spec(a.shape) for a in consts],
        out_specs=row,
        out_shape=jax.ShapeDtypeStruct((bsz, s, d), F32),
        scratch_shapes=[pltpu.VMEM((tm + FFN_HALO, tf), F32), pltpu.VMEM((FFN_HALO, 2 * D_FF), F32)],
        compiler_params=pltpu.CompilerParams(dimension_semantics=("arbitrary", "arbitrary"),
                                             vmem_limit_bytes=VMEM_LIMIT),
        name="conv_gated_mlp",
    )(x, mod_l, *consts)


def _rope_tables(positions):
    pos = positions.astype(F32)[..., None]
    bsz, s = positions.shape

    def cos_sin(dim):
        inv = 1.0 / (ROPE_THETA ** (jnp.arange(0, dim, 2, dtype=F32) / dim))
        ang = pos * inv
        return jnp.cos(ang), jnp.sin(ang)

    zeros = lambda w: jnp.zeros((bsz, s, w), F32)
    cr, sr = cos_sin(MLA_ROPE)
    hr = MLA_ROPE // 2
    pad = LANES - MLA_NOPE - MLA_ROPE
    c = jnp.concatenate([jnp.ones((bsz, s, MLA_NOPE), F32), cr, cr, zeros(pad)], -1)
    s_up = jnp.concatenate([zeros(MLA_NOPE), -sr, zeros(hr), zeros(pad)], -1)
    s_dn = jnp.concatenate([zeros(MLA_NOPE), zeros(hr), sr, zeros(pad)], -1)
    tab_k = jnp.concatenate([c, s_up, s_dn], -1)
    tab_q = tab_k * (1.0 / math.sqrt(MLA_NOPE + MLA_ROPE))
    cd, sd = cos_sin(DIFF_HD)
    hd = DIFF_HD // 2
    c = jnp.concatenate([cd, cd, cd, cd], -1)
    s_up = jnp.concatenate([-sd, zeros(hd), -sd, zeros(hd)], -1)
    s_dn = jnp.concatenate([zeros(hd), sd, zeros(hd), sd], -1)
    tab_d = jnp.concatenate([c, s_up, s_dn], -1)
    return tab_q, tab_k, tab_d


def _tile(s, want):
    return min(want, s)


def kernel(x, c, positions, ada_w, ada_b, norm1_g, w_in, q_norm_g, w_uq, kv_norm_g, w_ukv, w_out_mla, diff_lambda,
           diff_subln_g, w_out_diff, conv_dw_w, conv_dw_b, conv_ln_g, conv_ln_b, w_out_conv, w_o, norm2_g, w_up,
           ffn_conv_w, ffn_conv_b, w_down, final_g):
    bsz, s, d = x.shape
    depth = ada_w.shape[0]
    tm = _tile(s, 512)
    tq = _tile(s, 256)
    tk = tq
    tf = 256

    tab_q, tab_k, tab_d = _rope_tables(positions)
    mod = _modulation(c, ada_w, ada_b)

    o_q = 0
    o_kv = o_q + MLA_Q_RANK
    o_kr = o_kv + MLA_KV_RANK
    o_d = o_kr + MLA_ROPE
    o_c = o_d + 3 * DIFF_COLS
    o_g = o_c + 2 * CONV_CH
    w_in_b = w_in.astype(BF16)
    wq = w_in_b[:, :, o_q:o_kv]
    kr_pad = jnp.pad(w_in_b[:, :, o_kr:o_d], ((0, 0), (0, 0), (MLA_NOPE, LANES - MLA_NOPE - MLA_ROPE)))
    wkv = jnp.concatenate([w_in_b[:, :, o_kv:o_kr], kr_pad], -1)
    wd = w_in_b[:, :, o_d:o_c]
    wc = w_in_b[:, :, o_c:o_g]
    wg = w_in_b[:, :, o_g:]
    hq = MLA_NOPE + MLA_ROPE
    wuq = jnp.pad(w_uq.astype(BF16).reshape(depth, MLA_Q_RANK, MLA_HEADS, hq),
                  ((0, 0), (0, 0), (0, 0), (0, LANES - hq))).reshape(depth, MLA_Q_RANK, MLA_PAD_COLS)
    wukv_h = w_ukv.astype(BF16).reshape(depth, MLA_KV_RANK, MLA_HEADS, MLA_NOPE + MLA_V)
    wuk = jnp.pad(wukv_h[..., :MLA_NOPE], ((0, 0), (0, 0), (0, 0), (0, LANES - MLA_NOPE)))
    wukv = jnp.concatenate([wuk.reshape(depth, MLA_KV_RANK, MLA_PAD_COLS),
                            wukv_h[..., MLA_NOPE:].reshape(depth, MLA_KV_RANK, MLA_HEADS * MLA_V)], -1)
    woa, wob, woc, wo = (w.astype(BF16) for w in (w_out_mla, w_out_diff, w_out_conv, w_o))
    wup, wdn = w_up.astype(BF16), w_down.astype(BF16)
    vec = lambda a: a[:, None, :]
    g1, g2, qg, kvg, sg = vec(norm1_g), vec(norm2_g), vec(q_norm_g), vec(kv_norm_g), vec(diff_subln_g)
    cb, lng, lnb, fcb = vec(conv_dw_b), vec(conv_ln_g), vec(conv_ln_b), vec(ffn_conv_b)
    fg = final_g[None, :]

    for l in range(depth):
        lambda_init = 0.8 - 0.6 * math.exp(-0.3 * l)
        q, k, v, dq, dk, dv, u = _pre_mixer(x, mod[l], g1[l], wq[l], wkv[l], wd[l], wc[l], wuq[l], wukv[l],
                                            qg[l], kvg[l], tab_q, tab_k, tab_d, tm)
        o_a = _mla_attention(q, k, v, tq, tk)
        o_b = _diff_attention(dq, dk, dv, diff_lambda[l], sg[l], tq, tk, lambda_init)
        x = _post_mixer(x, mod[l], g1[l], o_a, o_b, u, wg[l], woa[l], wob[l], woc[l], wo[l],
                        conv_dw_w[l], cb[l], lng[l], lnb[l], tm)
        x = _ffn(x, mod[l], g2[l], wup[l], ffn_conv_w[l], fcb[l], wdn[l], fg, tm, tf, l == depth - 1)
    return x
```

```python
import functools
import math

import jax
import jax.numpy as jnp
from jax import lax
from jax.experimental import pallas as pl
from jax.experimental.pallas import tpu as pltpu

F32 = jnp.float32
BF16 = jnp.bfloat16

D_MODEL = 1024
MLA_HEADS = 8
MLA_NOPE = 64
MLA_ROPE = 32
MLA_V = 64
MLA_Q_RANK = 384
MLA_KV_RANK = 256
DIFF_HEADS = 4
DIFF_HD = 64
DIFF_VD = 2 * DIFF_HD
CONV_CH = 512
CONV_K = 31
D_FF = 2816
FFN_CONV_K = 3
ROPE_THETA = 10000.0
NORM_EPS = 1e-6
N_MOD = 6

LANES = 128
CONV_HALO = 32
FFN_HALO = 8
VMEM_LIMIT = 56 * 1024 * 1024
NEG_BIG = -1e30
LOG2E = math.log2(math.e)
ATTN_TQ = 256
ATTN_TK = 512

DIFF_COLS = DIFF_HEADS * 2 * DIFF_HD
MLA_PAD_COLS = MLA_HEADS * LANES


def _sigmoid(x):
    return 1.0 / (1.0 + jnp.exp(-x))


def _rms(x, g):
    return x * lax.rsqrt(jnp.mean(x * x, axis=-1, keepdims=True) + NORM_EPS) * g


def _apply_rot(x, c, s_up, s_dn, half):
    return x * c + pltpu.roll(x, LANES - half, 1) * s_up + pltpu.roll(x, half, 1) * s_dn


def _const_spec(shape):
    nd = len(shape)
    return pl.BlockSpec(shape, lambda *_: (0,) * nd, pipeline_mode=pl.Buffered(1))


def _mod_kernel(c_ref, w_ref, b_ref, o_ref):
    c = c_ref[...]
    c_act = (c * _sigmoid(c)).astype(BF16)
    o_ref[...] = jnp.dot(c_act, w_ref[...].astype(BF16), preferred_element_type=F32) + b_ref[...]


def _modulation(c, ada_w, ada_b):
    depth, d, _ = ada_w.shape
    bsz = c.shape[0]
    out = pl.pallas_call(
        _mod_kernel,
        grid=(depth, N_MOD),
        in_specs=[
            pl.BlockSpec((bsz, d), lambda l, j: (0, 0)),
            pl.BlockSpec((None, d, d), lambda l, j: (l, 0, j)),
            pl.BlockSpec((None, 1, d), lambda l, j: (l * N_MOD + j, 0, 0)),
        ],
        out_specs=pl.BlockSpec((None, None, bsz, d), lambda l, j: (l, j, 0, 0)),
        out_shape=jax.ShapeDtypeStruct((depth, N_MOD, bsz, d), F32),
        compiler_params=pltpu.CompilerParams(dimension_semantics=("arbitrary", "arbitrary"),
                                             vmem_limit_bytes=VMEM_LIMIT),
        name="adaln_mod",
    )(c, ada_w, ada_b.reshape(depth * N_MOD, 1, d))
    return out.transpose(0, 2, 1, 3)


def _pre_kernel(x_ref, mod_ref, g1_ref, wq_ref, wkv_ref, wd_ref, wc_ref, wuq_ref, wukv_ref, qg_ref, kvg_ref,
                tq_ref, tk_ref, td_ref, q_out, k_out, v_out, dq_out, dk_out, dv_out, u_out):
    x = x_ref[...]
    h = _rms(x, g1_ref[...]) * (1.0 + mod_ref[1:2, :]) + mod_ref[0:1, :]
    hb = h.astype(BF16)

    q_lat = jnp.dot(hb, wq_ref[...], preferred_element_type=F32)
    qn = _rms(q_lat, qg_ref[...]).astype(BF16)
    q = jnp.dot(qn, wuq_ref[...], preferred_element_type=F32)
    half = MLA_ROPE // 2
    cq, su, sd = tq_ref[:, 0:LANES], tq_ref[:, LANES:2 * LANES], tq_ref[:, 2 * LANES:3 * LANES]
    for hd in range(MLA_HEADS):
        blk = slice(hd * LANES, (hd + 1) * LANES)
        q_out[:, blk] = _apply_rot(q[:, blk], cq, su, sd, half).astype(BF16)

    kv_lat = jnp.dot(hb, wkv_ref[...], preferred_element_type=F32)
    kvn = _rms(kv_lat[:, :MLA_KV_RANK], kvg_ref[...]).astype(BF16)
    kv = jnp.dot(kvn, wukv_ref[...], preferred_element_type=F32)
    ck, su, sd = tk_ref[:, 0:LANES], tk_ref[:, LANES:2 * LANES], tk_ref[:, 2 * LANES:3 * LANES]
    k_pe = _apply_rot(kv_lat[:, MLA_KV_RANK:MLA_KV_RANK + LANES], ck, su, sd, half)
    for hd in range(MLA_HEADS):
        blk = slice(hd * LANES, (hd + 1) * LANES)
        k_out[:, blk] = (kv[:, blk] + k_pe).astype(BF16)
    v_out[...] = kv[:, MLA_PAD_COLS:].astype(BF16)

    d = jnp.dot(hb, wd_ref[...], preferred_element_type=F32)
    half = DIFF_HD // 2
    cd, su, sd = td_ref[:, 0:LANES], td_ref[:, LANES:2 * LANES], td_ref[:, 2 * LANES:3 * LANES]
    scale = LOG2E / math.sqrt(DIFF_HD)
    for hd in range(DIFF_HEADS):
        blk = slice(hd * LANES, (hd + 1) * LANES)
        kblk = slice(DIFF_COLS + hd * LANES, DIFF_COLS + (hd + 1) * LANES)
        dq_out[:, blk] = (_apply_rot(d[:, blk], cd, su, sd, half) * scale).astype(BF16)
        dk_out[:, blk] = _apply_rot(d[:, kblk], cd, su, sd, half).astype(BF16)
    dv_out[...] = d[:, 2 * DIFF_COLS:].astype(BF16)

    cv = jnp.dot(hb, wc_ref[...], preferred_element_type=F32)
    u_out[...] = cv[:, :CONV_CH] * _sigmoid(cv[:, CONV_CH:])


def _pre_mixer(x, mod_l, g1, wq, wkv, wd, wc, wuq, wukv, qg, kvg, tab_q, tab_k, tab_d, tm):
    bsz, s, d = x.shape
    row = lambda w: pl.BlockSpec((None, tm, w), lambda b, i: (b, i, 0))
    outs = [(MLA_PAD_COLS, BF16), (MLA_PAD_COLS, BF16), (MLA_HEADS * MLA_V, BF16),
            (DIFF_COLS, BF16), (DIFF_COLS, BF16), (DIFF_COLS, BF16), (CONV_CH, F32)]
    return pl.pallas_call(
        _pre_kernel,
        grid=(bsz, s // tm),
        in_specs=[row(d), pl.BlockSpec((None, N_MOD, d), lambda b, i: (b, 0, 0)), _const_spec(g1.shape),
                  _const_spec(wq.shape), _const_spec(wkv.shape), _const_spec(wd.shape), _const_spec(wc.shape),
                  _const_spec(wuq.shape), _const_spec(wukv.shape), _const_spec(qg.shape), _const_spec(kvg.shape),
                  row(3 * LANES), row(3 * LANES), row(3 * LANES)],
        out_specs=[row(w) for w, _ in outs],
        out_shape=[jax.ShapeDtypeStruct((bsz, s, w), dt) for w, dt in outs],
        compiler_params=pltpu.CompilerParams(dimension_semantics=("parallel", "parallel"),
                                             vmem_limit_bytes=VMEM_LIMIT),
        name="pre_mixer",
    )(x, mod_l, g1, wq, wkv, wd, wc, wuq, wukv, qg, kvg, tab_q, tab_k, tab_d)


def _flash_stacked(q, k_ref, v_ref, scratch, row0, tq, tk):
    s_buf, m_ref, l_ref, acc_ref = scratch
    rows2 = 2 * tq
    n_full = row0 // tk

    def scores(kb, slot):
        ks = pl.multiple_of(kb * tk, tk)
        s_buf[slot] = lax.dot_general(q, k_ref[pl.ds(ks, tk), :], (((1,), (1,)), ((), ())),
                                      preferred_element_type=F32)

    def update(kb, slot, masked):
        ks = pl.multiple_of(kb * tk, tk)
        s = s_buf[slot]
        if masked:
            r = lax.broadcasted_iota(jnp.int32, (rows2, tk), 0)
            pos = row0 + jnp.where(r >= tq, r - tq, r)
            cols = ks + lax.broadcasted_iota(jnp.int32, (rows2, tk), 1)
            s = jnp.where(cols <= pos, s, NEG_BIG)
        m = m_ref[...]
        m_new = jnp.maximum(m, jnp.max(s, axis=-1, keepdims=True))
        alpha = jnp.exp2(m - m_new)
        p = jnp.exp2(s - m_new)
        l_ref[...] = alpha * l_ref[...] + jnp.sum(p, axis=-1, keepdims=True)
        m_ref[...] = m_new
        pv = jnp.dot(p.astype(BF16), v_ref[pl.ds(ks, tk), :], preferred_element_type=F32)
        acc_ref[...] = alpha * acc_ref[...] + pv

    m_ref[...] = jnp.full(m_ref.shape, NEG_BIG, F32)
    l_ref[...] = jnp.zeros(l_ref.shape, F32)
    acc_ref[...] = jnp.zeros(acc_ref.shape, F32)
    scores(0, 0)

    def pair(jj, carry):
        kb = 2 * jj
        scores(kb + 1, 1)
        update(kb, 0, False)
        scores(kb + 2, 0)
        update(kb + 1, 1, False)
        return carry

    lax.fori_loop(0, n_full // 2, pair, 0)
    odd = n_full % 2 == 1

    @pl.when(odd)
    def _():
        scores(n_full, 1)
        update(n_full - 1, 0, False)
        update(n_full, 1, True)

    @pl.when(jnp.logical_not(odd))
    def _():
        update(n_full, 0, True)

    return acc_ref[...] / l_ref[...]


def _flash_scratch(tq, tk):
    rows2 = 2 * tq
    return [pltpu.VMEM((2, rows2, tk), F32), pltpu.VMEM((rows2, 1), F32), pltpu.VMEM((rows2, 1), F32),
            pltpu.VMEM((rows2, LANES), F32)]


def _mla_attn_kernel(q_ref, k_ref, v_ref, o_ref, *scratch, tq, tk):
    row0 = pl.program_id(2) * tq
    q = q_ref[...]
    lane = lax.broadcasted_iota(jnp.int32, q.shape, 1)
    zero = jnp.zeros_like(q)
    q2 = jnp.concatenate([jnp.where(lane < LANES, q, zero), jnp.where(lane < LANES, zero, q)], axis=0)
    o = _flash_stacked(q2, k_ref, v_ref, scratch, row0, tq, tk)
    lane = lax.broadcasted_iota(jnp.int32, (tq, LANES), 1)
    o_ref[...] = jnp.where(lane < MLA_V, o[:tq], o[tq:]).astype(BF16)


def _mla_attention(q, k, v):
    bsz, s, _ = q.shape
    tq, tk = _tile(s, ATTN_TQ), _tile(s, ATTN_TK)
    return pl.pallas_call(
        functools.partial(_mla_attn_kernel, tq=tq, tk=tk),
        grid=(bsz, MLA_HEADS // 2, s // tq),
        in_specs=[pl.BlockSpec((None, tq, 2 * LANES), lambda b, j, i: (b, i, j)),
                  pl.BlockSpec((None, s, 2 * LANES), lambda b, j, i: (b, 0, j)),
                  pl.BlockSpec((None, s, LANES), lambda b, j, i: (b, 0, j))],
        out_specs=pl.BlockSpec((None, tq, LANES), lambda b, j, i: (b, i, j)),
        out_shape=jax.ShapeDtypeStruct((bsz, s, MLA_HEADS * MLA_V), BF16),
        scratch_shapes=_flash_scratch(tq, tk),
        compiler_params=pltpu.CompilerParams(dimension_semantics=("parallel", "parallel", "parallel"),
                                             vmem_limit_bytes=VMEM_LIMIT),
        name="mla_attention",
    )(q, k, v)


def _diff_attn_kernel(q_ref, k_ref, v_ref, dl_ref, g_ref, o_ref, *scratch, tq, tk, lambda_init):
    row0 = pl.program_id(2) * tq
    q = q_ref[...]
    lane = lax.broadcasted_iota(jnp.int32, q.shape, 1)
    zero = jnp.zeros_like(q)
    q2 = jnp.concatenate([jnp.where(lane < DIFF_HD, q, zero), jnp.where(lane < DIFF_HD, zero, q)], axis=0)
    o = _flash_stacked(q2, k_ref, v_ref, scratch, row0, tq, tk)
    dl = dl_ref[...]
    lam = (jnp.exp(jnp.sum(dl[0:1, :] * dl[1:2, :], keepdims=True))
           - jnp.exp(jnp.sum(dl[2:3, :] * dl[3:4, :], keepdims=True)) + lambda_init)
    o = o[:tq] - lam * o[tq:]
    o_ref[...] = (_rms(o, g_ref[...]) * (1.0 - lambda_init)).astype(BF16)


def _diff_attention(q, k, v, dl, g, lambda_init):
    bsz, s, _ = q.shape
    tq, tk = _tile(s, ATTN_TQ), _tile(s, ATTN_TK)
    return pl.pallas_call(
        functools.partial(_diff_attn_kernel, tq=tq, tk=tk, lambda_init=lambda_init),
        grid=(bsz, DIFF_HEADS, s // tq),
        in_specs=[pl.BlockSpec((None, tq, LANES), lambda b, j, i: (b, i, j)),
                  pl.BlockSpec((None, s, LANES), lambda b, j, i: (b, 0, j)),
                  pl.BlockSpec((None, s, LANES), lambda b, j, i: (b, 0, j)),
                  pl.BlockSpec(dl.shape, lambda b, j, i: (0, 0)),
                  pl.BlockSpec(g.shape, lambda b, j, i: (0, 0))],
        out_specs=pl.BlockSpec((None, tq, LANES), lambda b, j, i: (b, i, j)),
        out_shape=jax.ShapeDtypeStruct((bsz, s, DIFF_HEADS * DIFF_VD), BF16),
        scratch_shapes=_flash_scratch(tq, tk),
        compiler_params=pltpu.CompilerParams(dimension_semantics=("parallel", "parallel", "parallel"),
                                             vmem_limit_bytes=VMEM_LIMIT),
        name="diff_attention",
    )(q, k, v, dl, g)


def _post_kernel(x_ref, mod_ref, g1_ref, oa_ref, ob_ref, u_ref, wg_ref, woa_ref, wob_ref, woc_ref, wo_ref,
                 cw_ref, cb_ref, lng_ref, lnb_ref, out_ref, ubuf, cbuf, *, tm, chunk):
    i = pl.program_id(1)

    @pl.when(i == 0)
    def _():
        ubuf[0:CONV_HALO, :] = jnp.zeros((CONV_HALO, CONV_CH), F32)

    @pl.when(i > 0)
    def _():
        ubuf[0:CONV_HALO, :] = ubuf[tm:tm + CONV_HALO, :]

    ubuf[CONV_HALO:tm + CONV_HALO, :] = u_ref[...]

    base = CONV_HALO - (CONV_K - 1)
    for c in range(tm // chunk):
        acc = jnp.broadcast_to(cb_ref[...], (chunk, CONV_CH))
        for j in range(CONV_K):
            r = c * chunk + base + j
            acc = acc + cw_ref[j:j + 1, :] * ubuf[r:r + chunk, :]
        cbuf[c * chunk:(c + 1) * chunk, :] = acc
    uc = cbuf[...]
    mu = jnp.mean(uc, axis=-1, keepdims=True)
    xc = uc - mu
    var = jnp.mean(xc * xc, axis=-1, keepdims=True)
    ln = xc * lax.rsqrt(var + NORM_EPS) * lng_ref[...] + lnb_ref[...]
    y_c = jnp.dot((ln * _sigmoid(ln)).astype(BF16), woc_ref[...], preferred_element_type=F32)

    x = x_ref[...]
    h = _rms(x, g1_ref[...]) * (1.0 + mod_ref[1:2, :]) + mod_ref[0:1, :]
    hb = h.astype(BF16)
    d = x.shape[-1]
    merged = _sigmoid(jnp.dot(hb, wg_ref[:, 2 * d:3 * d], preferred_element_type=F32)) * y_c
    y_a = jnp.dot(oa_ref[...], woa_ref[...], preferred_element_type=F32)
    merged = merged + _sigmoid(jnp.dot(hb, wg_ref[:, 0:d], preferred_element_type=F32)) * y_a
    y_b = jnp.dot(ob_ref[...], wob_ref[...], preferred_element_type=F32)
    merged = merged + _sigmoid(jnp.dot(hb, wg_ref[:, d:2 * d], preferred_element_type=F32)) * y_b
    out_ref[...] = x + mod_ref[2:3, :] * jnp.dot(merged.astype(BF16), wo_ref[...], preferred_element_type=F32)


def _post_mixer(x, mod_l, g1, o_a, o_b, u, wg, woa, wob, woc, wo, cw, cb, lng, lnb, tm):
    bsz, s, d = x.shape
    row = lambda w: pl.BlockSpec((None, tm, w), lambda b, i: (b, i, 0))
    consts = (g1,)
    weights = (wg, woa, wob, woc, wo, cw, cb, lng, lnb)
    return pl.pallas_call(
        functools.partial(_post_kernel, tm=tm, chunk=min(32, tm)),
        grid=(bsz, s // tm),
        in_specs=[row(d), pl.BlockSpec((None, N_MOD, d), lambda b, i: (b, 0, 0))]
                 + [_const_spec(a.shape) for a in consts]
                 + [row(o_a.shape[-1]), row(o_b.shape[-1]), row(CONV_CH)]
                 + [_const_spec(a.shape) for a in weights],
        out_specs=row(d),
        out_shape=jax.ShapeDtypeStruct((bsz, s, d), F32),
        scratch_shapes=[pltpu.VMEM((tm + CONV_HALO, CONV_CH), F32), pltpu.VMEM((tm, CONV_CH), F32)],
        compiler_params=pltpu.CompilerParams(dimension_semantics=("arbitrary", "arbitrary"),
                                             vmem_limit_bytes=VMEM_LIMIT),
        name="post_mixer",
    )(x, mod_l, g1, o_a, o_b, u, *weights)


def _ffn_kernel(x_ref, mod_ref, g2_ref, wup_ref, fcw_ref, fcb_ref, wdn_ref, fg_ref, out_ref, ext, carry,
                *, tm, tf, final):
    i = pl.program_id(1)

    @pl.when(i == 0)
    def _():
        carry[...] = jnp.zeros(carry.shape, F32)

    x = x_ref[...]
    h = _rms(x, g2_ref[...]) * (1.0 + mod_ref[4:5, :]) + mod_ref[3:4, :]
    hb = h.astype(BF16)
    acc = jnp.zeros(x.shape, F32)
    for c in range(D_FF // tf):
        halves = []
        for part in range(2):
            cols = slice(part * D_FF + c * tf, part * D_FF + (c + 1) * tf)
            up = jnp.dot(hb, wup_ref[:, cols], preferred_element_type=F32)
            ext[0:FFN_HALO, :] = carry[:, cols]
            ext[FFN_HALO:tm + FFN_HALO, :] = up
            carry[:, cols] = ext[tm:tm + FFN_HALO, :]
            y = fcb_ref[:, cols] + fcw_ref[2:3, cols] * ext[FFN_HALO:tm + FFN_HALO, :]
            y = y + fcw_ref[1:2, cols] * ext[FFN_HALO - 1:tm + FFN_HALO - 1, :]
            y = y + fcw_ref[0:1, cols] * ext[FFN_HALO - 2:tm + FFN_HALO - 2, :]
            halves.append(y)
        a, b = halves
        gated = (a * _sigmoid(a) * b).astype(BF16)
        acc = acc + jnp.dot(gated, wdn_ref[c * tf:(c + 1) * tf, :], preferred_element_type=F32)
    y = x + mod_ref[5:6, :] * acc
    if final:
        y = _rms(y, fg_ref[...])
    out_ref[...] = y


def _ffn(x, mod_l, g2, wup, fcw, fcb, wdn, fg, tm, tf, final):
    bsz, s, d = x.shape
    row = pl.BlockSpec((None, tm, d), lambda b, i: (b, i, 0))
    consts = (g2, wup, fcw, fcb, wdn, fg)
    return pl.pallas_call(
        functools.partial(_ffn_kernel, tm=tm, tf=tf, final=final),
        grid=(bsz, s // tm),
        in_specs=[row, pl.BlockSpec((None, N_MOD, d), lambda b, i: (b, 0, 0))]
                 + [_const_spec(a.shape) for a in consts],
        out_specs=row,
        out_shape=jax.ShapeDtypeStruct((bsz, s, d), F32),
        scratch_shapes=[pltpu.VMEM((tm + FFN_HALO, tf), F32), pltpu.VMEM((FFN_HALO, 2 * D_FF), F32)],
        compiler_params=pltpu.CompilerParams(dimension_semantics=("arbitrary", "arbitrary"),
                                             vmem_limit_bytes=VMEM_LIMIT),
        name="conv_gated_mlp",
    )(x, mod_l, *consts)


def _rope_tables(positions):
    pos = positions.astype(F32)[..., None]
    bsz, s = positions.shape

    def cos_sin(dim):
        inv = 1.0 / (ROPE_THETA ** (jnp.arange(0, dim, 2, dtype=F32) / dim))
        ang = pos * inv
        return jnp.cos(ang), jnp.sin(ang)

    zeros = lambda w: jnp.zeros((bsz, s, w), F32)
    cr, sr = cos_sin(MLA_ROPE)
    hr = MLA_ROPE // 2
    pad = LANES - MLA_NOPE - MLA_ROPE
    c = jnp.concatenate([jnp.ones((bsz, s, MLA_NOPE), F32), cr, cr, zeros(pad)], -1)
    s_up = jnp.concatenate([zeros(MLA_NOPE), -sr, zeros(hr), zeros(pad)], -1)
    s_dn = jnp.concatenate([zeros(MLA_NOPE), zeros(hr), sr, zeros(pad)], -1)
    tab_k = jnp.concatenate([c, s_up, s_dn], -1)
    tab_q = tab_k * (LOG2E / math.sqrt(MLA_NOPE + MLA_ROPE))
    cd, sd = cos_sin(DIFF_HD)
    hd = DIFF_HD // 2
    c = jnp.concatenate([cd, cd, cd, cd], -1)
    s_up = jnp.concatenate([-sd, zeros(hd), -sd, zeros(hd)], -1)
    s_dn = jnp.concatenate([zeros(hd), sd, zeros(hd), sd], -1)
    tab_d = jnp.concatenate([c, s_up, s_dn], -1)
    return tab_q, tab_k, tab_d


def _tile(s, want):
    return min(want, s)


def kernel(x, c, positions, ada_w, ada_b, norm1_g, w_in, q_norm_g, w_uq, kv_norm_g, w_ukv, w_out_mla, diff_lambda,
           diff_subln_g, w_out_diff, conv_dw_w, conv_dw_b, conv_ln_g, conv_ln_b, w_out_conv, w_o, norm2_g, w_up,
           ffn_conv_w, ffn_conv_b, w_down, final_g):
    bsz, s, d = x.shape
    depth = ada_w.shape[0]
    tm = _tile(s, 512)
    tf = 256

    tab_q, tab_k, tab_d = _rope_tables(positions)
    mod = _modulation(c, ada_w, ada_b)

    o_q = 0
    o_kv = o_q + MLA_Q_RANK
    o_kr = o_kv + MLA_KV_RANK
    o_d = o_kr + MLA_ROPE
    o_c = o_d + 3 * DIFF_COLS
    o_g = o_c + 2 * CONV_CH
    w_in_b = w_in.astype(BF16)
    wq = w_in_b[:, :, o_q:o_kv]
    kr_pad = jnp.pad(w_in_b[:, :, o_kr:o_d], ((0, 0), (0, 0), (MLA_NOPE, LANES - MLA_NOPE - MLA_ROPE)))
    wkv = jnp.concatenate([w_in_b[:, :, o_kv:o_kr], kr_pad], -1)
    wd = w_in_b[:, :, o_d:o_c]
    wc = w_in_b[:, :, o_c:o_g]
    wg = w_in_b[:, :, o_g:]
    hq = MLA_NOPE + MLA_ROPE
    wuq = jnp.pad(w_uq.astype(BF16).reshape(depth, MLA_Q_RANK, MLA_HEADS, hq),
                  ((0, 0), (0, 0), (0, 0), (0, LANES - hq))).reshape(depth, MLA_Q_RANK, MLA_PAD_COLS)
    wukv_h = w_ukv.astype(BF16).reshape(depth, MLA_KV_RANK, MLA_HEADS, MLA_NOPE + MLA_V)
    wuk = jnp.pad(wukv_h[..., :MLA_NOPE], ((0, 0), (0, 0), (0, 0), (0, LANES - MLA_NOPE)))
    wukv = jnp.concatenate([wuk.reshape(depth, MLA_KV_RANK, MLA_PAD_COLS),
                            wukv_h[..., MLA_NOPE:].reshape(depth, MLA_KV_RANK, MLA_HEADS * MLA_V)], -1)
    woa, wob, woc, wo = (w.astype(BF16) for w in (w_out_mla, w_out_diff, w_out_conv, w_o))
    wup, wdn = w_up.astype(BF16), w_down.astype(BF16)
    vec = lambda a: a[:, None, :]
    g1, g2, qg, kvg, sg = vec(norm1_g), vec(norm2_g), vec(q_norm_g), vec(kv_norm_g), vec(diff_subln_g)
    cb, lng, lnb, fcb = vec(conv_dw_b), vec(conv_ln_g), vec(conv_ln_b), vec(ffn_conv_b)
    fg = final_g[None, :]

    for l in range(depth):
        lambda_init = 0.8 - 0.6 * math.exp(-0.3 * l)
        q, k, v, dq, dk, dv, u = _pre_mixer(x, mod[l], g1[l], wq[l], wkv[l], wd[l], wc[l], wuq[l], wukv[l],
                                            qg[l], kvg[l], tab_q, tab_k, tab_d, tm)
        o_a = _mla_attention(q, k, v)
        o_b = _diff_attention(dq, dk, dv, diff_lambda[l], sg[l], lambda_init)
        x = _post_mixer(x, mod[l], g1[l], o_a, o_b, u, wg[l], woa[l], wob[l], woc[l], wo[l],
                        conv_dw_w[l], cb[l], lng[l], lnb[l], tm)
        x = _ffn(x, mod[l], g2[l], wup[l], ffn_conv_w[l], fcb[l], wdn[l], fg, tm, tf, l == depth - 1)
    return x
```

```python
import functools
import math

import jax
import jax.numpy as jnp
from jax import lax
from jax.experimental import pallas as pl
from jax.experimental.pallas import tpu as pltpu

F32 = jnp.float32
BF16 = jnp.bfloat16

D_MODEL = 1024
MLA_HEADS = 8
MLA_NOPE = 64
MLA_ROPE = 32
MLA_V = 64
MLA_Q_RANK = 384
MLA_KV_RANK = 256
DIFF_HEADS = 4
DIFF_HD = 64
DIFF_VD = 2 * DIFF_HD
CONV_CH = 512
CONV_K = 31
D_FF = 2816
FFN_CONV_K = 3
ROPE_THETA = 10000.0
NORM_EPS = 1e-6
N_MOD = 6

LANES = 128
SUBLANES = 8
CONV_HALO = 32
FFN_HALO = 8
VMEM_LIMIT = 56 * 1024 * 1024
NEG_BIG = -1e30
LOG2E = math.log2(math.e)
ATTN_TQ = 512
ATTN_TK = 512

DIFF_COLS = DIFF_HEADS * 2 * DIFF_HD
MLA_PAD_COLS = MLA_HEADS * LANES


def _sigmoid(x):
    return 0.5 * jnp.tanh(0.5 * x) + 0.5


def _rms(x, g):
    return x * lax.rsqrt(jnp.mean(x * x, axis=-1, keepdims=True) + NORM_EPS) * g


def _apply_rot(x, c, s_up, s_dn, half):
    return x * c + pltpu.roll(x, LANES - half, 1) * s_up + pltpu.roll(x, half, 1) * s_dn


def _const_spec(shape):
    nd = len(shape)
    return pl.BlockSpec(shape, lambda *_: (0,) * nd, pipeline_mode=pl.Buffered(1))


def _mod_kernel(c_ref, w_ref, b_ref, o_ref):
    c = c_ref[...]
    c_act = (c * _sigmoid(c)).astype(BF16)
    o_ref[...] = jnp.dot(c_act, w_ref[...].astype(BF16), preferred_element_type=F32) + b_ref[...]


def _modulation(c, ada_w, ada_b):
    depth, d, _ = ada_w.shape
    bsz = c.shape[0]
    out = pl.pallas_call(
        _mod_kernel,
        grid=(depth, N_MOD),
        in_specs=[
            pl.BlockSpec((bsz, d), lambda l, j: (0, 0)),
            pl.BlockSpec((None, d, d), lambda l, j: (l, 0, j)),
            pl.BlockSpec((None, 1, d), lambda l, j: (l * N_MOD + j, 0, 0)),
        ],
        out_specs=pl.BlockSpec((None, None, bsz, d), lambda l, j: (l, j, 0, 0)),
        out_shape=jax.ShapeDtypeStruct((depth, N_MOD, bsz, d), F32),
        compiler_params=pltpu.CompilerParams(dimension_semantics=("arbitrary", "arbitrary"),
                                             vmem_limit_bytes=VMEM_LIMIT),
        name="adaln_mod",
    )(c, ada_w, ada_b.reshape(depth * N_MOD, 1, d))
    return out.transpose(0, 2, 1, 3)


def _pre_kernel(x_ref, mod_ref, g1_ref, wq_ref, wkv_ref, wd_ref, wc_ref, wuq_ref, wukv_ref, qg_ref, kvg_ref,
                tq_ref, tk_ref, td_ref, q_out, k_out, v_out, dq_out, dk_out, dv_out, u_out):
    x = x_ref[...]
    h = _rms(x, g1_ref[...]) * (1.0 + mod_ref[1:2, :]) + mod_ref[0:1, :]
    hb = h.astype(BF16)

    q_lat = jnp.dot(hb, wq_ref[...], preferred_element_type=F32)
    qn = _rms(q_lat, qg_ref[...]).astype(BF16)
    q = jnp.dot(qn, wuq_ref[...], preferred_element_type=F32)
    half = MLA_ROPE // 2
    cq, su, sd = tq_ref[:, 0:LANES], tq_ref[:, LANES:2 * LANES], tq_ref[:, 2 * LANES:3 * LANES]
    for hd in range(MLA_HEADS):
        blk = slice(hd * LANES, (hd + 1) * LANES)
        q_out[:, blk] = _apply_rot(q[:, blk], cq, su, sd, half).astype(BF16)

    kv_lat = jnp.dot(hb, wkv_ref[...], preferred_element_type=F32)
    kvn = _rms(kv_lat[:, :MLA_KV_RANK], kvg_ref[...]).astype(BF16)
    kv = jnp.dot(kvn, wukv_ref[...], preferred_element_type=F32)
    ck, su, sd = tk_ref[:, 0:LANES], tk_ref[:, LANES:2 * LANES], tk_ref[:, 2 * LANES:3 * LANES]
    k_pe = _apply_rot(kv_lat[:, MLA_KV_RANK:MLA_KV_RANK + LANES], ck, su, sd, half)
    for hd in range(MLA_HEADS):
        blk = slice(hd * LANES, (hd + 1) * LANES)
        k_out[:, blk] = (kv[:, blk] + k_pe).astype(BF16)
    v_out[...] = kv[:, MLA_PAD_COLS:].astype(BF16)

    d = jnp.dot(hb, wd_ref[...], preferred_element_type=F32)
    half = DIFF_HD // 2
    cd, su, sd = td_ref[:, 0:LANES], td_ref[:, LANES:2 * LANES], td_ref[:, 2 * LANES:3 * LANES]
    scale = LOG2E / math.sqrt(DIFF_HD)
    for hd in range(DIFF_HEADS):
        blk = slice(hd * LANES, (hd + 1) * LANES)
        kblk = slice(DIFF_COLS + hd * LANES, DIFF_COLS + (hd + 1) * LANES)
        dq_out[:, blk] = (_apply_rot(d[:, blk], cd, su, sd, half) * scale).astype(BF16)
        dk_out[:, blk] = _apply_rot(d[:, kblk], cd, su, sd, half).astype(BF16)
    dv_out[...] = d[:, 2 * DIFF_COLS:].astype(BF16)

    cv = jnp.dot(hb, wc_ref[...], preferred_element_type=F32)
    u_out[...] = cv[:, :CONV_CH] * _sigmoid(cv[:, CONV_CH:])


def _pre_mixer(x, mod_l, g1, wq, wkv, wd, wc, wuq, wukv, qg, kvg, tab_q, tab_k, tab_d, tm):
    bsz, s, d = x.shape
    row = lambda w: pl.BlockSpec((None, tm, w), lambda b, i: (b, i, 0))
    outs = [(MLA_PAD_COLS, BF16), (MLA_PAD_COLS, BF16), (MLA_HEADS * MLA_V, BF16),
            (DIFF_COLS, BF16), (DIFF_COLS, BF16), (DIFF_COLS, BF16), (CONV_CH, F32)]
    return pl.pallas_call(
        _pre_kernel,
        grid=(bsz, s // tm),
        in_specs=[row(d), pl.BlockSpec((None, N_MOD, d), lambda b, i: (b, 0, 0)), _const_spec(g1.shape),
                  _const_spec(wq.shape), _const_spec(wkv.shape), _const_spec(wd.shape), _const_spec(wc.shape),
                  _const_spec(wuq.shape), _const_spec(wukv.shape), _const_spec(qg.shape), _const_spec(kvg.shape),
                  row(3 * LANES), row(3 * LANES), row(3 * LANES)],
        out_specs=[row(w) for w, _ in outs],
        out_shape=[jax.ShapeDtypeStruct((bsz, s, w), dt) for w, dt in outs],
        compiler_params=pltpu.CompilerParams(dimension_semantics=("parallel", "parallel"),
                                             vmem_limit_bytes=VMEM_LIMIT),
        name="pre_mixer",
    )(x, mod_l, g1, wq, wkv, wd, wc, wuq, wukv, qg, kvg, tab_q, tab_k, tab_d)


def _flash_stacked(q2t, k_ref, vt_ref, scratch, row0, tq, tk):
    s_buf, m_ref, l_ref, acc_ref = scratch
    rows2 = 2 * tq
    n_full = row0 // tk

    def scores(kb, slot):
        ks = pl.multiple_of(kb * tk, tk)
        s_buf[slot] = jnp.dot(k_ref[pl.ds(ks, tk), :], q2t, preferred_element_type=F32)

    def update(kb, slot, masked):
        s = s_buf[slot]
        if masked:
            c = lax.broadcasted_iota(jnp.int32, (tk, rows2), 1)
            pos = row0 + jnp.where(c >= tq, c - tq, c)
            keys = kb * tk + lax.broadcasted_iota(jnp.int32, (tk, rows2), 0)
            s = jnp.where(keys <= pos, s, NEG_BIG)
        m = m_ref[...]
        m_new = jnp.maximum(m, jnp.max(s, axis=0, keepdims=True))
        alpha = jnp.exp2(m - m_new)
        p = jnp.exp2(s - m_new)
        l_ref[...] = alpha * l_ref[...] + jnp.sum(p, axis=0, keepdims=True)
        m_ref[...] = m_new
        pv = jnp.dot(vt_ref[kb], p.astype(BF16), preferred_element_type=F32)
        acc_ref[...] = alpha * acc_ref[...] + pv

    m_ref[...] = jnp.full(m_ref.shape, NEG_BIG, F32)
    l_ref[...] = jnp.zeros(l_ref.shape, F32)
    acc_ref[...] = jnp.zeros(acc_ref.shape, F32)
    scores(0, 0)

    def pair(jj, carry):
        kb = 2 * jj
        scores(kb + 1, 1)
        update(kb, 0, False)
        scores(kb + 2, 0)
        update(kb + 1, 1, False)
        return carry

    lax.fori_loop(0, n_full // 2, pair, 0)
    odd = n_full % 2 == 1

    @pl.when(odd)
    def _():
        scores(n_full, 1)
        update(n_full - 1, 0, False)
        update(n_full, 1, True)

    @pl.when(jnp.logical_not(odd))
    def _():
        update(n_full, 0, True)

    return acc_ref[...] / l_ref[...]


def _flash_scratch(s, tq, tk):
    rows2 = 2 * tq
    return [pltpu.VMEM((s // tk, LANES, tk), BF16), pltpu.VMEM((2, tk, rows2), F32), pltpu.VMEM((1, rows2), F32),
            pltpu.VMEM((1, rows2), F32), pltpu.VMEM((LANES, rows2), F32)]


def _transpose_values(v_ref, vt_ref, tk):
    @pl.when(pl.program_id(2) == 0)
    def _():
        for kb in range(vt_ref.shape[0]):
            vt_ref[kb] = v_ref[kb * tk:(kb + 1) * tk, :].astype(F32).T.astype(BF16)


def _mla_attn_kernel(q_ref, k_ref, v_ref, o_ref, vt_ref, *scratch, tq, tk):
    row0 = pl.program_id(2) * tq
    _transpose_values(v_ref, vt_ref, tk)
    qt = q_ref[...].astype(F32).T.astype(BF16)
    zero = jnp.zeros((LANES, tq), BF16)
    q2t = jnp.concatenate([jnp.concatenate([qt[:LANES], zero], axis=1),
                           jnp.concatenate([zero, qt[LANES:]], axis=1)], axis=0)
    ot = _flash_stacked(q2t, k_ref, vt_ref, scratch, row0, tq, tk)
    feat = lax.broadcasted_iota(jnp.int32, (LANES, tq), 0)
    o_ref[...] = jnp.where(feat < MLA_V, ot[:, :tq], ot[:, tq:]).T.astype(BF16)


def _mla_attention(q, k, v):
    bsz, s, _ = q.shape
    tq, tk = _tile(s, ATTN_TQ), _tile(s, ATTN_TK)
    return pl.pallas_call(
        functools.partial(_mla_attn_kernel, tq=tq, tk=tk),
        grid=(bsz, MLA_HEADS // 2, s // tq),
        in_specs=[pl.BlockSpec((None, tq, 2 * LANES), lambda b, j, i: (b, i, j)),
                  pl.BlockSpec((None, s, 2 * LANES), lambda b, j, i: (b, 0, j)),
                  pl.BlockSpec((None, s, LANES), lambda b, j, i: (b, 0, j))],
        out_specs=pl.BlockSpec((None, tq, LANES), lambda b, j, i: (b, i, j)),
        out_shape=jax.ShapeDtypeStruct((bsz, s, MLA_HEADS * MLA_V), BF16),
        scratch_shapes=_flash_scratch(s, tq, tk),
        compiler_params=pltpu.CompilerParams(dimension_semantics=("arbitrary", "arbitrary", "arbitrary"),
                                             vmem_limit_bytes=VMEM_LIMIT),
        name="mla_attention",
    )(q, k, v)


def _diff_attn_kernel(q_ref, k_ref, v_ref, dl_ref, g_ref, o_ref, vt_ref, *scratch, tq, tk, lambda_init):
    row0 = pl.program_id(2) * tq
    _transpose_values(v_ref, vt_ref, tk)
    qt = q_ref[...].astype(F32).T.astype(BF16)
    feat = lax.broadcasted_iota(jnp.int32, qt.shape, 0)
    zero = jnp.zeros_like(qt)
    q2t = jnp.concatenate([jnp.where(feat < DIFF_HD, qt, zero), jnp.where(feat < DIFF_HD, zero, qt)], axis=1)
    ot = _flash_stacked(q2t, k_ref, vt_ref, scratch, row0, tq, tk)
    dl = dl_ref[...]
    lam = (jnp.exp(jnp.sum(dl[0:1, :] * dl[1:2, :], keepdims=True))
           - jnp.exp(jnp.sum(dl[2:3, :] * dl[3:4, :], keepdims=True)) + lambda_init)
    o = (ot[:, :tq] - lam * ot[:, tq:]).T
    o_ref[...] = (_rms(o, g_ref[...]) * (1.0 - lambda_init)).astype(BF16)


def _diff_attention(q, k, v, dl, g, lambda_init):
    bsz, s, _ = q.shape
    tq, tk = _tile(s, ATTN_TQ), _tile(s, ATTN_TK)
    return pl.pallas_call(
        functools.partial(_diff_attn_kernel, tq=tq, tk=tk, lambda_init=lambda_init),
        grid=(bsz, DIFF_HEADS, s // tq),
        in_specs=[pl.BlockSpec((None, tq, LANES), lambda b, j, i: (b, i, j)),
                  pl.BlockSpec((None, s, LANES), lambda b, j, i: (b, 0, j)),
                  pl.BlockSpec((None, s, LANES), lambda b, j, i: (b, 0, j)),
                  pl.BlockSpec(dl.shape, lambda b, j, i: (0, 0)),
                  pl.BlockSpec(g.shape, lambda b, j, i: (0, 0))],
        out_specs=pl.BlockSpec((None, tq, LANES), lambda b, j, i: (b, i, j)),
        out_shape=jax.ShapeDtypeStruct((bsz, s, DIFF_HEADS * DIFF_VD), BF16),
        scratch_shapes=_flash_scratch(s, tq, tk),
        compiler_params=pltpu.CompilerParams(dimension_semantics=("arbitrary", "arbitrary", "arbitrary"),
                                             vmem_limit_bytes=VMEM_LIMIT),
        name="diff_attention",
    )(q, k, v, dl, g)


def _post_kernel(x_ref, mod_ref, g1_ref, oa_ref, ob_ref, u_ref, wg_ref, woa_ref, wob_ref, woc_ref, wo_ref,
                 cw_ref, cb_ref, lng_ref, lnb_ref, out_ref, ubuf, cbuf, *, tm, chunk):
    i = pl.program_id(1)

    @pl.when(i == 0)
    def _():
        ubuf[0, 0:CONV_HALO, :] = jnp.zeros((CONV_HALO, CONV_CH), F32)

    @pl.when(i > 0)
    def _():
        ubuf[0, 0:CONV_HALO, :] = ubuf[0, tm:tm + CONV_HALO, :]

    rows = tm + CONV_HALO
    ubuf[0, CONV_HALO:rows, :] = u_ref[...]
    for s in range(1, SUBLANES):
        ubuf[s, SUBLANES:rows, :] = ubuf[0, SUBLANES - s:rows - s, :]

    for c in range(tm // chunk):
        acc = jnp.broadcast_to(cb_ref[...], (chunk, CONV_CH))
        for j in range(CONV_K):
            whole, s = divmod(CONV_K - 1 - j, SUBLANES)
            r = c * chunk + CONV_HALO - whole * SUBLANES
            acc = acc + cw_ref[j:j + 1, :] * ubuf[s, r:r + chunk, :]
        cbuf[c * chunk:(c + 1) * chunk, :] = acc
    uc = cbuf[...]
    mu = jnp.mean(uc, axis=-1, keepdims=True)
    xc = uc - mu
    var = jnp.mean(xc * xc, axis=-1, keepdims=True)
    ln = xc * lax.rsqrt(var + NORM_EPS) * lng_ref[...] + lnb_ref[...]
    y_c = jnp.dot((ln * _sigmoid(ln)).astype(BF16), woc_ref[...], preferred_element_type=F32)

    x = x_ref[...]
    h = _rms(x, g1_ref[...]) * (1.0 + mod_ref[1:2, :]) + mod_ref[0:1, :]
    hb = h.astype(BF16)
    d = x.shape[-1]
    merged = _sigmoid(jnp.dot(hb, wg_ref[:, 2 * d:3 * d], preferred_element_type=F32)) * y_c
    y_a = jnp.dot(oa_ref[...], woa_ref[...], preferred_element_type=F32)
    merged = merged + _sigmoid(jnp.dot(hb, wg_ref[:, 0:d], preferred_element_type=F32)) * y_a
    y_b = jnp.dot(ob_ref[...], wob_ref[...], preferred_element_type=F32)
    merged = merged + _sigmoid(jnp.dot(hb, wg_ref[:, d:2 * d], preferred_element_type=F32)) * y_b
    out_ref[...] = x + mod_ref[2:3, :] * jnp.dot(merged.astype(BF16), wo_ref[...], preferred_element_type=F32)


def _post_mixer(x, mod_l, g1, o_a, o_b, u, wg, woa, wob, woc, wo, cw, cb, lng, lnb, tm):
    bsz, s, d = x.shape
    row = lambda w: pl.BlockSpec((None, tm, w), lambda b, i: (b, i, 0))
    consts = (g1,)
    weights = (wg, woa, wob, woc, wo, cw, cb, lng, lnb)
    return pl.pallas_call(
        functools.partial(_post_kernel, tm=tm, chunk=min(32, tm)),
        grid=(bsz, s // tm),
        in_specs=[row(d), pl.BlockSpec((None, N_MOD, d), lambda b, i: (b, 0, 0))]
                 + [_const_spec(a.shape) for a in consts]
                 + [row(o_a.shape[-1]), row(o_b.shape[-1]), row(CONV_CH)]
                 + [_const_spec(a.shape) for a in weights],
        out_specs=row(d),
        out_shape=jax.ShapeDtypeStruct((bsz, s, d), F32),
        scratch_shapes=[pltpu.VMEM((SUBLANES, tm + CONV_HALO, CONV_CH), F32), pltpu.VMEM((tm, CONV_CH), F32)],
        compiler_params=pltpu.CompilerParams(dimension_semantics=("arbitrary", "arbitrary"),
                                             vmem_limit_bytes=VMEM_LIMIT),
        name="post_mixer",
    )(x, mod_l, g1, o_a, o_b, u, *weights)


def _ffn_kernel(x_ref, mod_ref, g2_ref, wup_ref, fcw_ref, fcb_ref, wdn_ref, fg_ref, out_ref, ext, carry,
                *, tm, tf, final):
    i = pl.program_id(1)

    @pl.when(i == 0)
    def _():
        carry[...] = jnp.zeros(carry.shape, F32)

    x = x_ref[...]
    h = _rms(x, g2_ref[...]) * (1.0 + mod_ref[4:5, :]) + mod_ref[3:4, :]
    hb = h.astype(BF16)
    acc = jnp.zeros(x.shape, F32)
    for c in range(D_FF // tf):
        halves = []
        for part in range(2):
            cols = slice(part * D_FF + c * tf, part * D_FF + (c + 1) * tf)
            up = jnp.dot(hb, wup_ref[:, cols], preferred_element_type=F32)
            ext[0:FFN_HALO, :] = carry[:, cols]
            ext[FFN_HALO:tm + FFN_HALO, :] = up
            carry[:, cols] = ext[tm:tm + FFN_HALO, :]
            y = fcb_ref[:, cols] + fcw_ref[2:3, cols] * ext[FFN_HALO:tm + FFN_HALO, :]
            y = y + fcw_ref[1:2, cols] * ext[FFN_HALO - 1:tm + FFN_HALO - 1, :]
            y = y + fcw_ref[0:1, cols] * ext[FFN_HALO - 2:tm + FFN_HALO - 2, :]
            halves.append(y)
        a, b = halves
        gated = (a * _sigmoid(a) * b).astype(BF16)
        acc = acc + jnp.dot(gated, wdn_ref[c * tf:(c + 1) * tf, :], preferred_element_type=F32)
    y = x + mod_ref[5:6, :] * acc
    if final:
        y = _rms(y, fg_ref[...])
    out_ref[...] = y


def _ffn(x, mod_l, g2, wup, fcw, fcb, wdn, fg, tm, tf, final):
    bsz, s, d = x.shape
    row = pl.BlockSpec((None, tm, d), lambda b, i: (b, i, 0))
    consts = (g2, wup, fcw, fcb, wdn, fg)
    return pl.pallas_call(
        functools.partial(_ffn_kernel, tm=tm, tf=tf, final=final),
        grid=(bsz, s // tm),
        in_specs=[row, pl.BlockSpec((None, N_MOD, d), lambda b, i: (b, 0, 0))]
                 + [_const_spec(a.shape) for a in consts],
        out_specs=row,
        out_shape=jax.ShapeDtypeStruct((bsz, s, d), F32),
        scratch_shapes=[pltpu.VMEM((tm + FFN_HALO, tf), F32), pltpu.VMEM((FFN_HALO, 2 * D_FF), F32)],
        compiler_params=pltpu.CompilerParams(dimension_semantics=("arbitrary", "arbitrary"),
                                             vmem_limit_bytes=VMEM_LIMIT),
        name="conv_gated_mlp",
    )(x, mod_l, *consts)


def _rope_tables(positions):
    pos = positions.astype(F32)[..., None]
    bsz, s = positions.shape

    def cos_sin(dim):
        inv = 1.0 / (ROPE_THETA ** (jnp.arange(0, dim, 2, dtype=F32) / dim))
        ang = pos * inv
        return jnp.cos(ang), jnp.sin(ang)

    zeros = lambda w: jnp.zeros((bsz, s, w), F32)
    cr, sr = cos_sin(MLA_ROPE)
    hr = MLA_ROPE // 2
    pad = LANES - MLA_NOPE - MLA_ROPE
    c = jnp.concatenate([jnp.ones((bsz, s, MLA_NOPE), F32), cr, cr, zeros(pad)], -1)
    s_up = jnp.concatenate([zeros(MLA_NOPE), -sr, zeros(hr), zeros(pad)], -1)
    s_dn = jnp.concatenate([zeros(MLA_NOPE), zeros(hr), sr, zeros(pad)], -1)
    tab_k = jnp.concatenate([c, s_up, s_dn], -1)
    tab_q = tab_k * (LOG2E / math.sqrt(MLA_NOPE + MLA_ROPE))
    cd, sd = cos_sin(DIFF_HD)
    hd = DIFF_HD // 2
    c = jnp.concatenate([cd, cd, cd, cd], -1)
    s_up = jnp.concatenate([-sd, zeros(hd), -sd, zeros(hd)], -1)
    s_dn = jnp.concatenate([zeros(hd), sd, zeros(hd), sd], -1)
    tab_d = jnp.concatenate([c, s_up, s_dn], -1)
    return tab_q, tab_k, tab_d


def _tile(s, want):
    return min(want, s)


def kernel(x, c, positions, ada_w, ada_b, norm1_g, w_in, q_norm_g, w_uq, kv_norm_g, w_ukv, w_out_mla, diff_lambda,
           diff_subln_g, w_out_diff, conv_dw_w, conv_dw_b, conv_ln_g, conv_ln_b, w_out_conv, w_o, norm2_g, w_up,
           ffn_conv_w, ffn_conv_b, w_down, final_g):
    bsz, s, d = x.shape
    depth = ada_w.shape[0]
    tm = _tile(s, 512)
    tf = 256

    tab_q, tab_k, tab_d = _rope_tables(positions)
    mod = _modulation(c, ada_w, ada_b)

    o_q = 0
    o_kv = o_q + MLA_Q_RANK
    o_kr = o_kv + MLA_KV_RANK
    o_d = o_kr + MLA_ROPE
    o_c = o_d + 3 * DIFF_COLS
    o_g = o_c + 2 * CONV_CH
    w_in_b = w_in.astype(BF16)
    wq = w_in_b[:, :, o_q:o_kv]
    kr_pad = jnp.pad(w_in_b[:, :, o_kr:o_d], ((0, 0), (0, 0), (MLA_NOPE, LANES - MLA_NOPE - MLA_ROPE)))
    wkv = jnp.concatenate([w_in_b[:, :, o_kv:o_kr], kr_pad], -1)
    wd = w_in_b[:, :, o_d:o_c]
    wc = w_in_b[:, :, o_c:o_g]
    wg = w_in_b[:, :, o_g:]
    hq = MLA_NOPE + MLA_ROPE
    wuq = jnp.pad(w_uq.astype(BF16).reshape(depth, MLA_Q_RANK, MLA_HEADS, hq),
                  ((0, 0), (0, 0), (0, 0), (0, LANES - hq))).reshape(depth, MLA_Q_RANK, MLA_PAD_COLS)
    wukv_h = w_ukv.astype(BF16).reshape(depth, MLA_KV_RANK, MLA_HEADS, MLA_NOPE + MLA_V)
    wuk = jnp.pad(wukv_h[..., :MLA_NOPE], ((0, 0), (0, 0), (0, 0), (0, LANES - MLA_NOPE)))
    wukv = jnp.concatenate([wuk.reshape(depth, MLA_KV_RANK, MLA_PAD_COLS),
                            wukv_h[..., MLA_NOPE:].reshape(depth, MLA_KV_RANK, MLA_HEADS * MLA_V)], -1)
    woa, wob, woc, wo = (w.astype(BF16) for w in (w_out_mla, w_out_diff, w_out_conv, w_o))
    wup, wdn = w_up.astype(BF16), w_down.astype(BF16)
    vec = lambda a: a[:, None, :]
    g1, g2, qg, kvg, sg = vec(norm1_g), vec(norm2_g), vec(q_norm_g), vec(kv_norm_g), vec(diff_subln_g)
    cb, lng, lnb, fcb = vec(conv_dw_b), vec(conv_ln_g), vec(conv_ln_b), vec(ffn_conv_b)
    fg = final_g[None, :]

    for l in range(depth):
        lambda_init = 0.8 - 0.6 * math.exp(-0.3 * l)
        q, k, v, dq, dk, dv, u = _pre_mixer(x, mod[l], g1[l], wq[l], wkv[l], wd[l], wc[l], wuq[l], wukv[l],
                                            qg[l], kvg[l], tab_q, tab_k, tab_d, tm)
        o_a = _mla_attention(q, k, v)
        o_b = _diff_attention(dq, dk, dv, diff_lambda[l], sg[l], lambda_init)
        x = _post_mixer(x, mod[l], g1[l], o_a, o_b, u, wg[l], woa[l], wob[l], woc[l], wo[l],
                        conv_dw_w[l], cb[l], lng[l], lnb[l], tm)
        x = _ffn(x, mod[l], g2[l], wup[l], ffn_conv_w[l], fcb[l], wdn[l], fg, tm, tf, l == depth - 1)
    return x
```

```python
import functools
import math

import jax
import jax.numpy as jnp
from jax import lax
from jax.experimental import pallas as pl
from jax.experimental.pallas import tpu as pltpu

F32 = jnp.float32
BF16 = jnp.bfloat16

D_MODEL = 1024
MLA_HEADS = 8
MLA_NOPE = 64
MLA_ROPE = 32
MLA_V = 64
MLA_Q_RANK = 384
MLA_KV_RANK = 256
DIFF_HEADS = 4
DIFF_HD = 64
DIFF_VD = 2 * DIFF_HD
CONV_CH = 512
CONV_K = 31
D_FF = 2816
FFN_CONV_K = 3
ROPE_THETA = 10000.0
NORM_EPS = 1e-6
N_MOD = 6

LANES = 128
SUBLANES = 8
CONV_HALO = 32
VMEM_LIMIT = 56 * 1024 * 1024
NEG_BIG = -1e30
LOG2E = math.log2(math.e)
ATTN_TQ = 512
ATTN_TK = 512

DIFF_COLS = DIFF_HEADS * 2 * DIFF_HD
MLA_PAD_COLS = MLA_HEADS * LANES


def _sigmoid(x):
    return 0.5 * jnp.tanh(0.5 * x) + 0.5


def _rms(x, g):
    return x * lax.rsqrt(jnp.mean(x * x, axis=-1, keepdims=True) + NORM_EPS) * g


def _apply_rot(x, c, s_up, s_dn, half):
    return x * c + pltpu.roll(x, LANES - half, 1) * s_up + pltpu.roll(x, half, 1) * s_dn


def _const_spec(shape):
    nd = len(shape)
    return pl.BlockSpec(shape, lambda *_: (0,) * nd, pipeline_mode=pl.Buffered(1))


def _mod_kernel(c_ref, w_ref, b_ref, o_ref):
    c = c_ref[...]
    c_act = (c * _sigmoid(c)).astype(BF16)
    o_ref[...] = jnp.dot(c_act, w_ref[...].astype(BF16), preferred_element_type=F32) + b_ref[...]


def _modulation(c, ada_w, ada_b):
    depth, d, _ = ada_w.shape
    bsz = c.shape[0]
    out = pl.pallas_call(
        _mod_kernel,
        grid=(depth, N_MOD),
        in_specs=[
            pl.BlockSpec((bsz, d), lambda l, j: (0, 0)),
            pl.BlockSpec((None, d, d), lambda l, j: (l, 0, j)),
            pl.BlockSpec((None, 1, d), lambda l, j: (l * N_MOD + j, 0, 0)),
        ],
        out_specs=pl.BlockSpec((None, None, bsz, d), lambda l, j: (l, j, 0, 0)),
        out_shape=jax.ShapeDtypeStruct((depth, N_MOD, bsz, d), F32),
        compiler_params=pltpu.CompilerParams(dimension_semantics=("arbitrary", "arbitrary"),
                                             vmem_limit_bytes=VMEM_LIMIT),
        name="adaln_mod",
    )(c, ada_w, ada_b.reshape(depth * N_MOD, 1, d))
    return out.transpose(0, 2, 1, 3)


def _pre_kernel(x_ref, mod_ref, g1_ref, wq_ref, wkv_ref, wd_ref, wc_ref, wuq_ref, wukv_ref, qg_ref, kvg_ref,
                tk_ref, td_ref, q_out, k_out, v_out, dq_out, dk_out, dv_out, u_out):
    x = x_ref[...]
    h = _rms(x, g1_ref[...]) * (1.0 + mod_ref[1:2, :]) + mod_ref[0:1, :]
    hb = h.astype(BF16)

    q_lat = jnp.dot(hb, wq_ref[...], preferred_element_type=F32)
    qn = _rms(q_lat, qg_ref[...]).astype(BF16)
    q = jnp.dot(qn, wuq_ref[...], preferred_element_type=F32)
    half = MLA_ROPE // 2
    ck, su, sd = tk_ref[:, 0:LANES], tk_ref[:, LANES:2 * LANES], tk_ref[:, 2 * LANES:3 * LANES]
    qscale = LOG2E / math.sqrt(MLA_NOPE + MLA_ROPE)
    cq, squ, sqd = ck * qscale, su * qscale, sd * qscale
    for hd in range(MLA_HEADS):
        blk = slice(hd * LANES, (hd + 1) * LANES)
        q_out[:, blk] = _apply_rot(q[:, blk], cq, squ, sqd, half).astype(BF16)

    kv_lat = jnp.dot(hb, wkv_ref[...], preferred_element_type=F32)
    kvn = _rms(kv_lat[:, :MLA_KV_RANK], kvg_ref[...]).astype(BF16)
    kv = jnp.dot(kvn, wukv_ref[...], preferred_element_type=F32)
    k_pe = _apply_rot(kv_lat[:, MLA_KV_RANK:MLA_KV_RANK + LANES], ck, su, sd, half)
    for hd in range(MLA_HEADS):
        blk = slice(hd * LANES, (hd + 1) * LANES)
        k_out[:, blk] = (kv[:, blk] + k_pe).astype(BF16)
    v_out[...] = kv[:, MLA_PAD_COLS:].astype(BF16)

    d = jnp.dot(hb, wd_ref[...], preferred_element_type=F32)
    half = DIFF_HD // 2
    cd, su, sd = td_ref[:, 0:LANES], td_ref[:, LANES:2 * LANES], td_ref[:, 2 * LANES:3 * LANES]
    scale = LOG2E / math.sqrt(DIFF_HD)
    for hd in range(DIFF_HEADS):
        blk = slice(hd * LANES, (hd + 1) * LANES)
        kblk = slice(DIFF_COLS + hd * LANES, DIFF_COLS + (hd + 1) * LANES)
        dq_out[:, blk] = (_apply_rot(d[:, blk], cd, su, sd, half) * scale).astype(BF16)
        dk_out[:, blk] = _apply_rot(d[:, kblk], cd, su, sd, half).astype(BF16)
    dv_out[...] = d[:, 2 * DIFF_COLS:].astype(BF16)

    cv = jnp.dot(hb, wc_ref[...], preferred_element_type=F32)
    u_out[...] = cv[:, :CONV_CH] * _sigmoid(cv[:, CONV_CH:])


def _pre_mixer(x, mod_l, g1, wq, wkv, wd, wc, wuq, wukv, qg, kvg, tab_k, tab_d, tm):
    bsz, s, d = x.shape
    row = lambda w: pl.BlockSpec((None, tm, w), lambda b, i: (b, i, 0))
    outs = [(MLA_PAD_COLS, BF16), (MLA_PAD_COLS, BF16), (MLA_HEADS * MLA_V, BF16),
            (DIFF_COLS, BF16), (DIFF_COLS, BF16), (DIFF_COLS, BF16), (CONV_CH, F32)]
    return pl.pallas_call(
        _pre_kernel,
        grid=(bsz, s // tm),
        in_specs=[row(d), pl.BlockSpec((None, N_MOD, d), lambda b, i: (b, 0, 0)), _const_spec(g1.shape),
                  _const_spec(wq.shape), _const_spec(wkv.shape), _const_spec(wd.shape), _const_spec(wc.shape),
                  _const_spec(wuq.shape), _const_spec(wukv.shape), _const_spec(qg.shape), _const_spec(kvg.shape),
                  row(3 * LANES), row(3 * LANES)],
        out_specs=[row(w) for w, _ in outs],
        out_shape=[jax.ShapeDtypeStruct((bsz, s, w), dt) for w, dt in outs],
        compiler_params=pltpu.CompilerParams(dimension_semantics=("parallel", "parallel"),
                                             vmem_limit_bytes=VMEM_LIMIT),
        name="pre_mixer",
    )(x, mod_l, g1, wq, wkv, wd, wc, wuq, wukv, qg, kvg, tab_k, tab_d)


def _flash_stacked(q2t, k_ref, vt_ref, scratch, row0, tq, tk):
    s_buf, m_ref, l_ref, acc_ref = scratch
    rows2 = 2 * tq
    n_full = row0 // tk

    def scores(kb, slot):
        ks = pl.multiple_of(kb * tk, tk)
        s_buf[slot] = jnp.dot(k_ref[pl.ds(ks, tk), :], q2t, preferred_element_type=F32)

    def update(kb, slot, masked):
        s = s_buf[slot]
        if masked:
            c = lax.broadcasted_iota(jnp.int32, (tk, rows2), 1)
            pos = row0 + jnp.where(c >= tq, c - tq, c)
            keys = kb * tk + lax.broadcasted_iota(jnp.int32, (tk, rows2), 0)
            s = jnp.where(keys <= pos, s, NEG_BIG)
        m = m_ref[...]
        m_new = jnp.maximum(m, jnp.max(s, axis=0, keepdims=True))
        alpha = jnp.exp2(m - m_new)
        p = jnp.exp2(s - m_new)
        l_ref[...] = alpha * l_ref[...] + jnp.sum(p, axis=0, keepdims=True)
        m_ref[...] = m_new
        pv = jnp.dot(vt_ref[kb], p.astype(BF16), preferred_element_type=F32)
        acc_ref[...] = alpha * acc_ref[...] + pv

    m_ref[...] = jnp.full(m_ref.shape, NEG_BIG, F32)
    l_ref[...] = jnp.zeros(l_ref.shape, F32)
    acc_ref[...] = jnp.zeros(acc_ref.shape, F32)
    scores(0, 0)

    def pair(jj, carry):
        kb = 2 * jj
        scores(kb + 1, 1)
        update(kb, 0, False)
        scores(kb + 2, 0)
        update(kb + 1, 1, False)
        return carry

    lax.fori_loop(0, n_full // 2, pair, 0)
    odd = n_full % 2 == 1

    @pl.when(odd)
    def _():
        scores(n_full, 1)
        update(n_full - 1, 0, False)
        update(n_full, 1, True)

    @pl.when(jnp.logical_not(odd))
    def _():
        update(n_full, 0, True)

    return acc_ref[...] / l_ref[...]


def _flash_scratch(s, tq, tk):
    rows2 = 2 * tq
    return [pltpu.VMEM((s // tk, LANES, tk), BF16), pltpu.VMEM((2, tk, rows2), F32), pltpu.VMEM((1, rows2), F32),
            pltpu.VMEM((1, rows2), F32), pltpu.VMEM((LANES, rows2), F32)]


def _transpose_values(v_ref, vt_ref, tk):
    @pl.when(pl.program_id(2) == 0)
    def _():
        for kb in range(vt_ref.shape[0]):
            vt_ref[kb] = v_ref[kb * tk:(kb + 1) * tk, :].astype(F32).T.astype(BF16)


def _mla_attn_kernel(q_ref, k_ref, v_ref, o_ref, vt_ref, *scratch, tq, tk):
    row0 = pl.program_id(2) * tq
    _transpose_values(v_ref, vt_ref, tk)
    qt = q_ref[...].astype(F32).T.astype(BF16)
    zero = jnp.zeros((LANES, tq), BF16)
    q2t = jnp.concatenate([jnp.concatenate([qt[:LANES], zero], axis=1),
                           jnp.concatenate([zero, qt[LANES:]], axis=1)], axis=0)
    ot = _flash_stacked(q2t, k_ref, vt_ref, scratch, row0, tq, tk)
    feat = lax.broadcasted_iota(jnp.int32, (LANES, tq), 0)
    o_ref[...] = jnp.where(feat < MLA_V, ot[:, :tq], ot[:, tq:]).T.astype(BF16)


def _mla_attention(q, k, v):
    bsz, s, _ = q.shape
    tq, tk = _tile(s, ATTN_TQ), _tile(s, ATTN_TK)
    return pl.pallas_call(
        functools.partial(_mla_attn_kernel, tq=tq, tk=tk),
        grid=(bsz, MLA_HEADS // 2, s // tq),
        in_specs=[pl.BlockSpec((None, tq, 2 * LANES), lambda b, j, i: (b, i, j)),
                  pl.BlockSpec((None, s, 2 * LANES), lambda b, j, i: (b, 0, j)),
                  pl.BlockSpec((None, s, LANES), lambda b, j, i: (b, 0, j))],
        out_specs=pl.BlockSpec((None, tq, LANES), lambda b, j, i: (b, i, j)),
        out_shape=jax.ShapeDtypeStruct((bsz, s, MLA_HEADS * MLA_V), BF16),
        scratch_shapes=_flash_scratch(s, tq, tk),
        compiler_params=pltpu.CompilerParams(dimension_semantics=("arbitrary", "arbitrary", "arbitrary"),
                                             vmem_limit_bytes=VMEM_LIMIT),
        name="mla_attention",
    )(q, k, v)


def _diff_attn_kernel(q_ref, k_ref, v_ref, dl_ref, g_ref, o_ref, vt_ref, *scratch, tq, tk, lambda_init):
    row0 = pl.program_id(2) * tq
    _transpose_values(v_ref, vt_ref, tk)
    qt = q_ref[...].astype(F32).T.astype(BF16)
    feat = lax.broadcasted_iota(jnp.int32, qt.shape, 0)
    zero = jnp.zeros_like(qt)
    q2t = jnp.concatenate([jnp.where(feat < DIFF_HD, qt, zero), jnp.where(feat < DIFF_HD, zero, qt)], axis=1)
    ot = _flash_stacked(q2t, k_ref, vt_ref, scratch, row0, tq, tk)
    dl = dl_ref[...]
    lam = (jnp.exp(jnp.sum(dl[0:1, :] * dl[1:2, :], keepdims=True))
           - jnp.exp(jnp.sum(dl[2:3, :] * dl[3:4, :], keepdims=True)) + lambda_init)
    o = (ot[:, :tq] - lam * ot[:, tq:]).T
    o_ref[...] = (_rms(o, g_ref[...]) * (1.0 - lambda_init)).astype(BF16)


def _diff_attention(q, k, v, dl, g, lambda_init):
    bsz, s, _ = q.shape
    tq, tk = _tile(s, ATTN_TQ), _tile(s, ATTN_TK)
    return pl.pallas_call(
        functools.partial(_diff_attn_kernel, tq=tq, tk=tk, lambda_init=lambda_init),
        grid=(bsz, DIFF_HEADS, s // tq),
        in_specs=[pl.BlockSpec((None, tq, LANES), lambda b, j, i: (b, i, j)),
                  pl.BlockSpec((None, s, LANES), lambda b, j, i: (b, 0, j)),
                  pl.BlockSpec((None, s, LANES), lambda b, j, i: (b, 0, j)),
                  pl.BlockSpec(dl.shape, lambda b, j, i: (0, 0)),
                  pl.BlockSpec(g.shape, lambda b, j, i: (0, 0))],
        out_specs=pl.BlockSpec((None, tq, LANES), lambda b, j, i: (b, i, j)),
        out_shape=jax.ShapeDtypeStruct((bsz, s, DIFF_HEADS * DIFF_VD), BF16),
        scratch_shapes=_flash_scratch(s, tq, tk),
        compiler_params=pltpu.CompilerParams(dimension_semantics=("arbitrary", "arbitrary", "arbitrary"),
                                             vmem_limit_bytes=VMEM_LIMIT),
        name="diff_attention",
    )(q, k, v, dl, g)


def _post_kernel(x_ref, mod_ref, g1_ref, oa_ref, ob_ref, u_ref, wg_ref, woa_ref, wob_ref, woc_ref, wo_ref,
                 cw_ref, cb_ref, lng_ref, lnb_ref, out_ref, ubuf, cbuf, *, tm, chunk):
    i = pl.program_id(1)

    @pl.when(i == 0)
    def _():
        ubuf[0, 0:CONV_HALO, :] = jnp.zeros((CONV_HALO, CONV_CH), F32)

    @pl.when(i > 0)
    def _():
        ubuf[0, 0:CONV_HALO, :] = ubuf[0, tm:tm + CONV_HALO, :]

    rows = tm + CONV_HALO
    ubuf[0, CONV_HALO:rows, :] = u_ref[...]
    for s in range(1, SUBLANES):
        ubuf[s, SUBLANES:rows, :] = ubuf[0, SUBLANES - s:rows - s, :]

    def conv_chunks(lo, hi):
        for c in range(lo, hi):
            acc = jnp.broadcast_to(cb_ref[...], (chunk, CONV_CH))
            for j in range(CONV_K):
                whole, s = divmod(CONV_K - 1 - j, SUBLANES)
                r = c * chunk + CONV_HALO - whole * SUBLANES
                acc = acc + cw_ref[j:j + 1, :] * ubuf[s, r:r + chunk, :]
            cbuf[c * chunk:(c + 1) * chunk, :] = acc

    x = x_ref[...]
    h = _rms(x, g1_ref[...]) * (1.0 + mod_ref[1:2, :]) + mod_ref[0:1, :]
    hb = h.astype(BF16)
    d = x.shape[-1]
    n_chunks = tm // chunk
    quarter = -(-n_chunks // 4)
    y_a = jnp.dot(oa_ref[...], woa_ref[...], preferred_element_type=F32)
    conv_chunks(0, min(quarter, n_chunks))
    merged = _sigmoid(jnp.dot(hb, wg_ref[:, 0:d], preferred_element_type=F32)) * y_a
    conv_chunks(min(quarter, n_chunks), min(2 * quarter, n_chunks))
    y_b = jnp.dot(ob_ref[...], wob_ref[...], preferred_element_type=F32)
    conv_chunks(min(2 * quarter, n_chunks), min(3 * quarter, n_chunks))
    merged = merged + _sigmoid(jnp.dot(hb, wg_ref[:, d:2 * d], preferred_element_type=F32)) * y_b
    conv_chunks(min(3 * quarter, n_chunks), n_chunks)
    gate_c = _sigmoid(jnp.dot(hb, wg_ref[:, 2 * d:3 * d], preferred_element_type=F32))

    uc = cbuf[...]
    mu = jnp.mean(uc, axis=-1, keepdims=True)
    xc = uc - mu
    var = jnp.mean(xc * xc, axis=-1, keepdims=True)
    ln = xc * lax.rsqrt(var + NORM_EPS) * lng_ref[...] + lnb_ref[...]
    y_c = jnp.dot((ln * _sigmoid(ln)).astype(BF16), woc_ref[...], preferred_element_type=F32)
    merged = merged + gate_c * y_c
    out_ref[...] = x + mod_ref[2:3, :] * jnp.dot(merged.astype(BF16), wo_ref[...], preferred_element_type=F32)


def _post_mixer(x, mod_l, g1, o_a, o_b, u, wg, woa, wob, woc, wo, cw, cb, lng, lnb, tm):
    bsz, s, d = x.shape
    row = lambda w: pl.BlockSpec((None, tm, w), lambda b, i: (b, i, 0))
    consts = (g1,)
    weights = (wg, woa, wob, woc, wo, cw, cb, lng, lnb)
    return pl.pallas_call(
        functools.partial(_post_kernel, tm=tm, chunk=min(32, tm)),
        grid=(bsz, s // tm),
        in_specs=[row(d), pl.BlockSpec((None, N_MOD, d), lambda b, i: (b, 0, 0))]
                 + [_const_spec(a.shape) for a in consts]
                 + [row(o_a.shape[-1]), row(o_b.shape[-1]), row(CONV_CH)]
                 + [_const_spec(a.shape) for a in weights],
        out_specs=row(d),
        out_shape=jax.ShapeDtypeStruct((bsz, s, d), F32),
        scratch_shapes=[pltpu.VMEM((SUBLANES, tm + CONV_HALO, CONV_CH), F32), pltpu.VMEM((tm, CONV_CH), F32)],
        compiler_params=pltpu.CompilerParams(dimension_semantics=("arbitrary", "arbitrary"),
                                             vmem_limit_bytes=VMEM_LIMIT),
        name="post_mixer",
    )(x, mod_l, g1, o_a, o_b, u, *weights)


def _shift_rows(u, prev, k):
    rolled = pltpu.roll(u, k, 0)
    first = lax.broadcasted_iota(jnp.int32, prev.shape, 0) < k
    head = jnp.where(first, pltpu.roll(prev, k, 0), rolled[:SUBLANES])
    return jnp.concatenate([head, rolled[SUBLANES:]], axis=0)


def _ffn_kernel(x_ref, mod_ref, g2_ref, wup_ref, fcw_ref, fcb_ref, wdn_ref, fg_ref, out_ref, carry, gbuf,
                *, tm, tf, final):
    i = pl.program_id(1)

    @pl.when(i == 0)
    def _():
        carry[...] = jnp.zeros(carry.shape, F32)

    x = x_ref[...]
    h = _rms(x, g2_ref[...]) * (1.0 + mod_ref[4:5, :]) + mod_ref[3:4, :]
    hb = h.astype(BF16)
    n_chunks = D_FF // tf

    def cols(part, c):
        return slice(part * D_FF + c * tf, part * D_FF + (c + 1) * tf)

    def up_proj(c):
        return [jnp.dot(hb, wup_ref[:, cols(part, c)], preferred_element_type=F32) for part in range(2)]

    ups = up_proj(0)
    for c in range(n_chunks):
        ups_next = up_proj(c + 1) if c + 1 < n_chunks else None
        halves = []
        for part in range(2):
            cs = cols(part, c)
            u = ups[part]
            prev = carry[:, cs]
            carry[:, cs] = u[tm - SUBLANES:tm, :]
            y = fcb_ref[:, cs] + fcw_ref[2:3, cs] * u
            y = y + fcw_ref[1:2, cs] * _shift_rows(u, prev, 1)
            y = y + fcw_ref[0:1, cs] * _shift_rows(u, prev, 2)
            halves.append(y)
        a, b = halves
        gbuf[:, c * tf:(c + 1) * tf] = (a * _sigmoid(a) * b).astype(BF16)
        ups = ups_next
    y = x + mod_ref[5:6, :] * jnp.dot(gbuf[...], wdn_ref[...], preferred_element_type=F32)
    if final:
        y = _rms(y, fg_ref[...])
    out_ref[...] = y


def _ffn(x, mod_l, g2, wup, fcw, fcb, wdn, fg, tm, tf, final):
    bsz, s, d = x.shape
    row = pl.BlockSpec((None, tm, d), lambda b, i: (b, i, 0))
    consts = (g2, wup, fcw, fcb, wdn, fg)
    return pl.pallas_call(
        functools.partial(_ffn_kernel, tm=tm, tf=tf, final=final),
        grid=(bsz, s // tm),
        in_specs=[row, pl.BlockSpec((None, N_MOD, d), lambda b, i: (b, 0, 0))]
                 + [_const_spec(a.shape) for a in consts],
        out_specs=row,
        out_shape=jax.ShapeDtypeStruct((bsz, s, d), F32),
        scratch_shapes=[pltpu.VMEM((SUBLANES, 2 * D_FF), F32), pltpu.VMEM((tm, D_FF), BF16)],
        compiler_params=pltpu.CompilerParams(dimension_semantics=("arbitrary", "arbitrary"),
                                             vmem_limit_bytes=VMEM_LIMIT),
        name="conv_gated_mlp",
    )(x, mod_l, *consts)


def _rope_tables(positions):
    pos = positions.astype(F32)[..., None]
    bsz, s = positions.shape

    def cos_sin(dim):
        inv = 1.0 / (ROPE_THETA ** (jnp.arange(0, dim, 2, dtype=F32) / dim))
        ang = pos * inv
        return jnp.cos(ang), jnp.sin(ang)

    zeros = lambda w: jnp.zeros((bsz, s, w), F32)
    cr, sr = cos_sin(MLA_ROPE)
    hr = MLA_ROPE // 2
    pad = LANES - MLA_NOPE - MLA_ROPE
    c = jnp.concatenate([jnp.ones((bsz, s, MLA_NOPE), F32), cr, cr, zeros(pad)], -1)
    s_up = jnp.concatenate([zeros(MLA_NOPE), -sr, zeros(hr), zeros(pad)], -1)
    s_dn = jnp.concatenate([zeros(MLA_NOPE), zeros(hr), sr, zeros(pad)], -1)
    tab_k = jnp.concatenate([c, s_up, s_dn], -1)
    cd, sd = cos_sin(DIFF_HD)
    hd = DIFF_HD // 2
    c = jnp.concatenate([cd, cd, cd, cd], -1)
    s_up = jnp.concatenate([-sd, zeros(hd), -sd, zeros(hd)], -1)
    s_dn = jnp.concatenate([zeros(hd), sd, zeros(hd), sd], -1)
    tab_d = jnp.concatenate([c, s_up, s_dn], -1)
    return tab_k, tab_d


def _tile(s, want):
    return min(want, s)


def kernel(x, c, positions, ada_w, ada_b, norm1_g, w_in, q_norm_g, w_uq, kv_norm_g, w_ukv, w_out_mla, diff_lambda,
           diff_subln_g, w_out_diff, conv_dw_w, conv_dw_b, conv_ln_g, conv_ln_b, w_out_conv, w_o, norm2_g, w_up,
           ffn_conv_w, ffn_conv_b, w_down, final_g):
    bsz, s, d = x.shape
    depth = ada_w.shape[0]
    tm = _tile(s, 512)
    tf = 256

    tab_k, tab_d = _rope_tables(positions)
    mod = _modulation(c, ada_w, ada_b)

    o_q = 0
    o_kv = o_q + MLA_Q_RANK
    o_kr = o_kv + MLA_KV_RANK
    o_d = o_kr + MLA_ROPE
    o_c = o_d + 3 * DIFF_COLS
    o_g = o_c + 2 * CONV_CH
    w_in_b = w_in.astype(BF16)
    wq = w_in_b[:, :, o_q:o_kv]
    kr_pad = jnp.pad(w_in_b[:, :, o_kr:o_d], ((0, 0), (0, 0), (MLA_NOPE, LANES - MLA_NOPE - MLA_ROPE)))
    wkv = jnp.concatenate([w_in_b[:, :, o_kv:o_kr], kr_pad], -1)
    wd = w_in_b[:, :, o_d:o_c]
    wc = w_in_b[:, :, o_c:o_g]
    wg = w_in_b[:, :, o_g:]
    hq = MLA_NOPE + MLA_ROPE
    wuq = jnp.pad(w_uq.astype(BF16).reshape(depth, MLA_Q_RANK, MLA_HEADS, hq),
                  ((0, 0), (0, 0), (0, 0), (0, LANES - hq))).reshape(depth, MLA_Q_RANK, MLA_PAD_COLS)
    wukv_h = w_ukv.astype(BF16).reshape(depth, MLA_KV_RANK, MLA_HEADS, MLA_NOPE + MLA_V)
    wuk = jnp.pad(wukv_h[..., :MLA_NOPE], ((0, 0), (0, 0), (0, 0), (0, LANES - MLA_NOPE)))
    wukv = jnp.concatenate([wuk.reshape(depth, MLA_KV_RANK, MLA_PAD_COLS),
                            wukv_h[..., MLA_NOPE:].reshape(depth, MLA_KV_RANK, MLA_HEADS * MLA_V)], -1)
    woa, wob, woc, wo = (w.astype(BF16) for w in (w_out_mla, w_out_diff, w_out_conv, w_o))
    wup, wdn = w_up.astype(BF16), w_down.astype(BF16)
    vec = lambda a: a[:, None, :]
    g1, g2, qg, kvg, sg = vec(norm1_g), vec(norm2_g), vec(q_norm_g), vec(kv_norm_g), vec(diff_subln_g)
    cb, lng, lnb, fcb = vec(conv_dw_b), vec(conv_ln_g), vec(conv_ln_b), vec(ffn_conv_b)
    fg = final_g[None, :]

    for l in range(depth):
        lambda_init = 0.8 - 0.6 * math.exp(-0.3 * l)
        q, k, v, dq, dk, dv, u = _pre_mixer(x, mod[l], g1[l], wq[l], wkv[l], wd[l], wc[l], wuq[l], wukv[l],
                                            qg[l], kvg[l], tab_k, tab_d, tm)
        o_a = _mla_attention(q, k, v)
        o_b = _diff_attention(dq, dk, dv, diff_lambda[l], sg[l], lambda_init)
        x = _post_mixer(x, mod[l], g1[l], o_a, o_b, u, wg[l], woa[l], wob[l], woc[l], wo[l],
                        conv_dw_w[l], cb[l], lng[l], lnb[l], tm)
        x = _ffn(x, mod[l], g2[l], wup[l], ffn_conv_w[l], fcb[l], wdn[l], fg, tm, tf, l == depth - 1)
    return x
```

```python
import functools
import math

import jax
import jax.numpy as jnp
import numpy as np
from jax import lax
from jax.experimental import pallas as pl
from jax.experimental.pallas import tpu as pltpu

F32 = jnp.float32
BF16 = jnp.bfloat16

D_MODEL = 1024
MLA_HEADS = 8
MLA_NOPE = 64
MLA_ROPE = 32
MLA_V = 64
MLA_Q_RANK = 384
MLA_KV_RANK = 256
DIFF_HEADS = 4
DIFF_HD = 64
DIFF_VD = 2 * DIFF_HD
CONV_CH = 512
CONV_K = 31
D_FF = 2816
FFN_CONV_K = 3
ROPE_THETA = 10000.0
NORM_EPS = 1e-6
N_MOD = 6

LANES = 128
SUBLANES = 8
CONV_HALO = 32
VMEM_LIMIT = 56 * 1024 * 1024
NEG_BIG = -1e30
LOG2E = math.log2(math.e)
ATTN_TQ = 512
ATTN_TK = 512
ROW_TILE = 512
FFN_COL_CHUNK = 256
CONV_ROW_CHUNK = 32

DIFF_COLS = DIFF_HEADS * 2 * DIFF_HD
MLA_PAD_COLS = MLA_HEADS * LANES


def _sigmoid(x):
    return 0.5 * jnp.tanh(0.5 * x) + 0.5


def _rms(x, g):
    return x * lax.rsqrt(jnp.mean(x * x, axis=-1, keepdims=True) + NORM_EPS) * g


def _apply_rot(x, c, s_up, s_dn, half):
    return x * c + pltpu.roll(x, LANES - half, 1) * s_up + pltpu.roll(x, half, 1) * s_dn


def _const_spec(shape):
    nd = len(shape)
    return pl.BlockSpec(shape, lambda *_: (0,) * nd, pipeline_mode=pl.Buffered(1))


def _mod_kernel(c_ref, w_ref, b_ref, o_ref):
    c = c_ref[...]
    c_act = (c * _sigmoid(c)).astype(BF16)
    o_ref[...] = jnp.dot(c_act, w_ref[...].astype(BF16), preferred_element_type=F32) + b_ref[...]


def _modulation(c, ada_w, ada_b):
    depth, d, _ = ada_w.shape
    bsz = c.shape[0]
    out = pl.pallas_call(
        _mod_kernel,
        grid=(depth, N_MOD),
        in_specs=[
            pl.BlockSpec((bsz, d), lambda l, j: (0, 0)),
            pl.BlockSpec((None, d, d), lambda l, j: (l, 0, j)),
            pl.BlockSpec((None, 1, d), lambda l, j: (l * N_MOD + j, 0, 0)),
        ],
        out_specs=pl.BlockSpec((None, None, bsz, d), lambda l, j: (l, j, 0, 0)),
        out_shape=jax.ShapeDtypeStruct((depth, N_MOD, bsz, d), F32),
        compiler_params=pltpu.CompilerParams(dimension_semantics=("arbitrary", "arbitrary"),
                                             vmem_limit_bytes=VMEM_LIMIT),
        name="adaln_mod",
    )(c, ada_w, ada_b.reshape(depth * N_MOD, 1, d))
    return out.transpose(0, 2, 1, 3)


def _pre_kernel(x_ref, mod_ref, g1_ref, wq_ref, wkv_ref, wd_ref, wc_ref, wuq_ref, wukv_ref, qg_ref, kvg_ref,
                tk_ref, td_ref, q_out, k_out, v_out, dq_out, dk_out, dv_out, u_out):
    x = x_ref[...]
    h = _rms(x, g1_ref[...]) * (1.0 + mod_ref[1:2, :]) + mod_ref[0:1, :]
    hb = h.astype(BF16)

    q_lat = jnp.dot(hb, wq_ref[...], preferred_element_type=F32)
    qn = _rms(q_lat, qg_ref[...]).astype(BF16)
    q = jnp.dot(qn, wuq_ref[...], preferred_element_type=F32)
    half = MLA_ROPE // 2
    ck, su, sd = tk_ref[:, 0:LANES], tk_ref[:, LANES:2 * LANES], tk_ref[:, 2 * LANES:3 * LANES]
    qscale = LOG2E / math.sqrt(MLA_NOPE + MLA_ROPE)
    cq, squ, sqd = ck * qscale, su * qscale, sd * qscale
    for hd in range(MLA_HEADS):
        blk = slice(hd * LANES, (hd + 1) * LANES)
        q_out[:, blk] = _apply_rot(q[:, blk], cq, squ, sqd, half).astype(BF16)

    kv_lat = jnp.dot(hb, wkv_ref[...], preferred_element_type=F32)
    kvn = _rms(kv_lat[:, :MLA_KV_RANK], kvg_ref[...]).astype(BF16)
    kv = jnp.dot(kvn, wukv_ref[...], preferred_element_type=F32)
    k_pe = _apply_rot(kv_lat[:, MLA_KV_RANK:MLA_KV_RANK + LANES], ck, su, sd, half)
    for hd in range(MLA_HEADS):
        blk = slice(hd * LANES, (hd + 1) * LANES)
        k_out[:, blk] = (kv[:, blk] + k_pe).astype(BF16)
    v_out[...] = kv[:, MLA_PAD_COLS:].astype(BF16)

    d = jnp.dot(hb, wd_ref[...], preferred_element_type=F32)
    half = DIFF_HD // 2
    cd, su, sd = td_ref[:, 0:LANES], td_ref[:, LANES:2 * LANES], td_ref[:, 2 * LANES:3 * LANES]
    scale = LOG2E / math.sqrt(DIFF_HD)
    for hd in range(DIFF_HEADS):
        blk = slice(hd * LANES, (hd + 1) * LANES)
        kblk = slice(DIFF_COLS + hd * LANES, DIFF_COLS + (hd + 1) * LANES)
        dq_out[:, blk] = (_apply_rot(d[:, blk], cd, su, sd, half) * scale).astype(BF16)
        dk_out[:, blk] = _apply_rot(d[:, kblk], cd, su, sd, half).astype(BF16)
    dv_out[...] = d[:, 2 * DIFF_COLS:].astype(BF16)

    cv = jnp.dot(hb, wc_ref[...], preferred_element_type=F32)
    u_out[...] = cv[:, :CONV_CH] * _sigmoid(cv[:, CONV_CH:])


def _pre_mixer(x, mod_l, g1, wq, wkv, wd, wc, wuq, wukv, qg, kvg, tab_k, tab_d, tm):
    bsz, s, d = x.shape
    row = lambda w: pl.BlockSpec((None, tm, w), lambda b, i: (b, i, 0))
    outs = [(MLA_PAD_COLS, BF16), (MLA_PAD_COLS, BF16), (MLA_HEADS * MLA_V, BF16),
            (DIFF_COLS, BF16), (DIFF_COLS, BF16), (DIFF_COLS, BF16), (CONV_CH, F32)]
    return pl.pallas_call(
        _pre_kernel,
        grid=(bsz, s // tm),
        in_specs=[row(d), pl.BlockSpec((None, N_MOD, d), lambda b, i: (b, 0, 0)), _const_spec(g1.shape),
                  _const_spec(wq.shape), _const_spec(wkv.shape), _const_spec(wd.shape), _const_spec(wc.shape),
                  _const_spec(wuq.shape), _const_spec(wukv.shape), _const_spec(qg.shape), _const_spec(kvg.shape),
                  row(3 * LANES), row(3 * LANES)],
        out_specs=[row(w) for w, _ in outs],
        out_shape=[jax.ShapeDtypeStruct((bsz, s, w), dt) for w, dt in outs],
        compiler_params=pltpu.CompilerParams(dimension_semantics=("parallel", "parallel"),
                                             vmem_limit_bytes=VMEM_LIMIT),
        name="pre_mixer",
    )(x, mod_l, g1, wq, wkv, wd, wc, wuq, wukv, qg, kvg, tab_k, tab_d)


def _flash_stacked(q2t, k_ref, vt_ref, scratch, row0, tq, tk):
    s_buf, m_ref, l_ref, acc_ref = scratch
    rows2 = 2 * tq
    n_full = row0 // tk

    def scores(kb, slot):
        ks = pl.multiple_of(kb * tk, tk)
        s_buf[slot] = jnp.dot(k_ref[pl.ds(ks, tk), :], q2t, preferred_element_type=F32)

    def update(kb, slot, masked):
        s = s_buf[slot]
        if masked:
            c = lax.broadcasted_iota(jnp.int32, (tk, rows2), 1)
            pos = row0 + jnp.where(c >= tq, c - tq, c)
            keys = kb * tk + lax.broadcasted_iota(jnp.int32, (tk, rows2), 0)
            s = jnp.where(keys <= pos, s, NEG_BIG)
        m = m_ref[...]
        m_new = jnp.maximum(m, jnp.max(s, axis=0, keepdims=True))
        alpha = jnp.exp2(m - m_new)
        p = jnp.exp2(s - m_new)
        l_ref[...] = alpha * l_ref[...] + jnp.sum(p, axis=0, keepdims=True)
        m_ref[...] = m_new
        pv = jnp.dot(vt_ref[kb], p.astype(BF16), preferred_element_type=F32)
        acc_ref[...] = alpha * acc_ref[...] + pv

    m_ref[...] = jnp.full(m_ref.shape, NEG_BIG, F32)
    l_ref[...] = jnp.zeros(l_ref.shape, F32)
    acc_ref[...] = jnp.zeros(acc_ref.shape, F32)
    scores(0, 0)

    def pair(jj, carry):
        kb = 2 * jj
        scores(kb + 1, 1)
        update(kb, 0, False)
        scores(kb + 2, 0)
        update(kb + 1, 1, False)
        return carry

    lax.fori_loop(0, n_full // 2, pair, 0)
    odd = n_full % 2 == 1

    @pl.when(odd)
    def _():
        scores(n_full, 1)
        update(n_full - 1, 0, False)
        update(n_full, 1, True)

    @pl.when(jnp.logical_not(odd))
    def _():
        update(n_full, 0, True)

    return acc_ref[...] / l_ref[...]


def _flash_scratch(s, tq, tk):
    rows2 = 2 * tq
    return [pltpu.VMEM((s // tk, LANES, tk), BF16), pltpu.VMEM((2, tk, rows2), F32), pltpu.VMEM((1, rows2), F32),
            pltpu.VMEM((1, rows2), F32), pltpu.VMEM((LANES, rows2), F32)]


def _transpose_values(v_ref, vt_ref, tk):
    @pl.when(pl.program_id(2) == 0)
    def _():
        for kb in range(vt_ref.shape[0]):
            vt_ref[kb] = v_ref[kb * tk:(kb + 1) * tk, :].astype(F32).T.astype(BF16)


def _mla_attn_kernel(q_ref, k_ref, v_ref, o_ref, vt_ref, *scratch, tq, tk):
    row0 = pl.program_id(2) * tq
    _transpose_values(v_ref, vt_ref, tk)
    qt = q_ref[...].astype(F32).T.astype(BF16)
    zero = jnp.zeros((LANES, tq), BF16)
    q2t = jnp.concatenate([jnp.concatenate([qt[:LANES], zero], axis=1),
                           jnp.concatenate([zero, qt[LANES:]], axis=1)], axis=0)
    ot = _flash_stacked(q2t, k_ref, vt_ref, scratch, row0, tq, tk)
    feat = lax.broadcasted_iota(jnp.int32, (LANES, tq), 0)
    o_ref[...] = jnp.where(feat < MLA_V, ot[:, :tq], ot[:, tq:]).T.astype(BF16)


def _mla_attention(q, k, v):
    bsz, s, _ = q.shape
    tq, tk = _tile(s, ATTN_TQ), _tile(s, ATTN_TK)
    return pl.pallas_call(
        functools.partial(_mla_attn_kernel, tq=tq, tk=tk),
        grid=(bsz, MLA_HEADS // 2, s // tq),
        in_specs=[pl.BlockSpec((None, tq, 2 * LANES), lambda b, j, i: (b, i, j)),
                  pl.BlockSpec((None, s, 2 * LANES), lambda b, j, i: (b, 0, j)),
                  pl.BlockSpec((None, s, LANES), lambda b, j, i: (b, 0, j))],
        out_specs=pl.BlockSpec((None, tq, LANES), lambda b, j, i: (b, i, j)),
        out_shape=jax.ShapeDtypeStruct((bsz, s, MLA_HEADS * MLA_V), BF16),
        scratch_shapes=_flash_scratch(s, tq, tk),
        compiler_params=pltpu.CompilerParams(dimension_semantics=("arbitrary", "arbitrary", "arbitrary"),
                                             vmem_limit_bytes=VMEM_LIMIT),
        name="mla_attention",
    )(q, k, v)


def _diff_attn_kernel(q_ref, k_ref, v_ref, dl_ref, g_ref, o_ref, vt_ref, *scratch, tq, tk, lambda_init):
    row0 = pl.program_id(2) * tq
    _transpose_values(v_ref, vt_ref, tk)
    qt = q_ref[...].astype(F32).T.astype(BF16)
    feat = lax.broadcasted_iota(jnp.int32, qt.shape, 0)
    zero = jnp.zeros_like(qt)
    q2t = jnp.concatenate([jnp.where(feat < DIFF_HD, qt, zero), jnp.where(feat < DIFF_HD, zero, qt)], axis=1)
    ot = _flash_stacked(q2t, k_ref, vt_ref, scratch, row0, tq, tk)
    dl = dl_ref[...]
    lam = (jnp.exp(jnp.sum(dl[0:1, :] * dl[1:2, :], keepdims=True))
           - jnp.exp(jnp.sum(dl[2:3, :] * dl[3:4, :], keepdims=True)) + lambda_init)
    o = (ot[:, :tq] - lam * ot[:, tq:]).T
    o_ref[...] = (_rms(o, g_ref[...]) * (1.0 - lambda_init)).astype(BF16)


def _diff_attention(q, k, v, dl, g, lambda_init):
    bsz, s, _ = q.shape
    tq, tk = _tile(s, ATTN_TQ), _tile(s, ATTN_TK)
    return pl.pallas_call(
        functools.partial(_diff_attn_kernel, tq=tq, tk=tk, lambda_init=lambda_init),
        grid=(bsz, DIFF_HEADS, s // tq),
        in_specs=[pl.BlockSpec((None, tq, LANES), lambda b, j, i: (b, i, j)),
                  pl.BlockSpec((None, s, LANES), lambda b, j, i: (b, 0, j)),
                  pl.BlockSpec((None, s, LANES), lambda b, j, i: (b, 0, j)),
                  pl.BlockSpec(dl.shape, lambda b, j, i: (0, 0)),
                  pl.BlockSpec(g.shape, lambda b, j, i: (0, 0))],
        out_specs=pl.BlockSpec((None, tq, LANES), lambda b, j, i: (b, i, j)),
        out_shape=jax.ShapeDtypeStruct((bsz, s, DIFF_HEADS * DIFF_VD), BF16),
        scratch_shapes=_flash_scratch(s, tq, tk),
        compiler_params=pltpu.CompilerParams(dimension_semantics=("arbitrary", "arbitrary", "arbitrary"),
                                             vmem_limit_bytes=VMEM_LIMIT),
        name="diff_attention",
    )(q, k, v, dl, g)


def _post_kernel(x_ref, mod_ref, g1_ref, oa_ref, ob_ref, u_ref, wg_ref, woa_ref, wob_ref, woc_ref, wo_ref,
                 cw_ref, cb_ref, lng_ref, lnb_ref, out_ref, ubuf, cbuf, *, tm, chunk):
    i = pl.program_id(1)

    @pl.when(i == 0)
    def _():
        ubuf[0, 0:CONV_HALO, :] = jnp.zeros((CONV_HALO, CONV_CH), F32)

    @pl.when(i > 0)
    def _():
        ubuf[0, 0:CONV_HALO, :] = ubuf[0, tm:tm + CONV_HALO, :]

    rows = tm + CONV_HALO
    ubuf[0, CONV_HALO:rows, :] = u_ref[...]
    for s in range(1, SUBLANES):
        ubuf[s, SUBLANES:rows, :] = ubuf[0, SUBLANES - s:rows - s, :]

    def conv_chunks(lo, hi):
        for c in range(lo, hi):
            acc = jnp.broadcast_to(cb_ref[...], (chunk, CONV_CH))
            for j in range(CONV_K):
                whole, s = divmod(CONV_K - 1 - j, SUBLANES)
                r = c * chunk + CONV_HALO - whole * SUBLANES
                acc = acc + cw_ref[j:j + 1, :] * ubuf[s, r:r + chunk, :]
            cbuf[c * chunk:(c + 1) * chunk, :] = acc

    x = x_ref[...]
    h = _rms(x, g1_ref[...]) * (1.0 + mod_ref[1:2, :]) + mod_ref[0:1, :]
    hb = h.astype(BF16)
    d = x.shape[-1]
    n_chunks = tm // chunk
    quarter = -(-n_chunks // 4)
    y_a = jnp.dot(oa_ref[...], woa_ref[...], preferred_element_type=F32)
    conv_chunks(0, min(quarter, n_chunks))
    merged = _sigmoid(jnp.dot(hb, wg_ref[:, 0:d], preferred_element_type=F32)) * y_a
    conv_chunks(min(quarter, n_chunks), min(2 * quarter, n_chunks))
    y_b = jnp.dot(ob_ref[...], wob_ref[...], preferred_element_type=F32)
    conv_chunks(min(2 * quarter, n_chunks), min(3 * quarter, n_chunks))
    merged = merged + _sigmoid(jnp.dot(hb, wg_ref[:, d:2 * d], preferred_element_type=F32)) * y_b
    conv_chunks(min(3 * quarter, n_chunks), n_chunks)
    gate_c = _sigmoid(jnp.dot(hb, wg_ref[:, 2 * d:3 * d], preferred_element_type=F32))

    uc = cbuf[...]
    mu = jnp.mean(uc, axis=-1, keepdims=True)
    xc = uc - mu
    var = jnp.mean(xc * xc, axis=-1, keepdims=True)
    ln = xc * lax.rsqrt(var + NORM_EPS) * lng_ref[...] + lnb_ref[...]
    y_c = jnp.dot((ln * _sigmoid(ln)).astype(BF16), woc_ref[...], preferred_element_type=F32)
    merged = merged + gate_c * y_c
    out_ref[...] = x + mod_ref[2:3, :] * jnp.dot(merged.astype(BF16), wo_ref[...], preferred_element_type=F32)


def _post_mixer(x, mod_l, g1, o_a, o_b, u, wg, woa, wob, woc, wo, cw, cb, lng, lnb, tm):
    bsz, s, d = x.shape
    row = lambda w: pl.BlockSpec((None, tm, w), lambda b, i: (b, i, 0))
    consts = (g1,)
    weights = (wg, woa, wob, woc, wo, cw, cb, lng, lnb)
    return pl.pallas_call(
        functools.partial(_post_kernel, tm=tm, chunk=min(CONV_ROW_CHUNK, tm)),
        grid=(bsz, s // tm),
        in_specs=[row(d), pl.BlockSpec((None, N_MOD, d), lambda b, i: (b, 0, 0))]
                 + [_const_spec(a.shape) for a in consts]
                 + [row(o_a.shape[-1]), row(o_b.shape[-1]), row(CONV_CH)]
                 + [_const_spec(a.shape) for a in weights],
        out_specs=row(d),
        out_shape=jax.ShapeDtypeStruct((bsz, s, d), F32),
        scratch_shapes=[pltpu.VMEM((SUBLANES, tm + CONV_HALO, CONV_CH), F32), pltpu.VMEM((tm, CONV_CH), F32)],
        compiler_params=pltpu.CompilerParams(dimension_semantics=("arbitrary", "arbitrary"),
                                             vmem_limit_bytes=VMEM_LIMIT),
        name="post_mixer",
    )(x, mod_l, g1, o_a, o_b, u, *weights)


def _shift_rows(u, prev, k):
    rolled = pltpu.roll(u, k, 0)
    first = lax.broadcasted_iota(jnp.int32, prev.shape, 0) < k
    head = jnp.where(first, pltpu.roll(prev, k, 0), rolled[:SUBLANES])
    return jnp.concatenate([head, rolled[SUBLANES:]], axis=0)


def _ffn_kernel(x_ref, mod_ref, g2_ref, wup_ref, fcw_ref, fcb_ref, wdn_ref, fg_ref, out_ref, carry, gbuf,
                *, tm, tf, final):
    i = pl.program_id(1)

    @pl.when(i == 0)
    def _():
        carry[...] = jnp.zeros(carry.shape, F32)

    x = x_ref[...]
    h = _rms(x, g2_ref[...]) * (1.0 + mod_ref[4:5, :]) + mod_ref[3:4, :]
    hb = h.astype(BF16)
    n_chunks = D_FF // tf

    def cols(part, c):
        return slice(part * D_FF + c * tf, part * D_FF + (c + 1) * tf)

    def up_proj(c):
        return [jnp.dot(hb, wup_ref[:, cols(part, c)], preferred_element_type=F32) for part in range(2)]

    ups = up_proj(0)
    for c in range(n_chunks):
        ups_next = up_proj(c + 1) if c + 1 < n_chunks else None
        halves = []
        for part in range(2):
            cs = cols(part, c)
            u = ups[part]
            prev = carry[:, cs]
            carry[:, cs] = u[tm - SUBLANES:tm, :]
            y = fcb_ref[:, cs] + fcw_ref[2:3, cs] * u
            y = y + fcw_ref[1:2, cs] * _shift_rows(u, prev, 1)
            y = y + fcw_ref[0:1, cs] * _shift_rows(u, prev, 2)
            halves.append(y)
        a, b = halves
        gbuf[:, c * tf:(c + 1) * tf] = (a * _sigmoid(a) * b).astype(BF16)
        ups = ups_next
    y = x + mod_ref[5:6, :] * jnp.dot(gbuf[...], wdn_ref[...], preferred_element_type=F32)
    if final:
        y = _rms(y, fg_ref[...])
    out_ref[...] = y


def _ffn(x, mod_l, g2, wup, fcw, fcb, wdn, fg, tm, tf, final):
    bsz, s, d = x.shape
    row = pl.BlockSpec((None, tm, d), lambda b, i: (b, i, 0))
    consts = (g2, wup, fcw, fcb, wdn, fg)
    return pl.pallas_call(
        functools.partial(_ffn_kernel, tm=tm, tf=tf, final=final),
        grid=(bsz, s // tm),
        in_specs=[row, pl.BlockSpec((None, N_MOD, d), lambda b, i: (b, 0, 0))]
                 + [_const_spec(a.shape) for a in consts],
        out_specs=row,
        out_shape=jax.ShapeDtypeStruct((bsz, s, d), F32),
        scratch_shapes=[pltpu.VMEM((SUBLANES, 2 * D_FF), F32), pltpu.VMEM((tm, D_FF), BF16)],
        compiler_params=pltpu.CompilerParams(dimension_semantics=("arbitrary", "arbitrary"),
                                             vmem_limit_bytes=VMEM_LIMIT),
        name="conv_gated_mlp",
    )(x, mod_l, *consts)


def _lane_table(pos, inv, idx, cos_coef, sin_coef, const):
    ang = pos * inv[idx]
    return (jnp.asarray(const, F32) + jnp.asarray(cos_coef, F32) * jnp.cos(ang)
            + jnp.asarray(sin_coef, F32) * jnp.sin(ang))


def _rope_tables(positions):
    pos = positions.astype(F32)[..., None]
    lane = np.arange(3 * LANES)
    sec, w = lane // LANES, lane % LANES

    def inv_freq(dim):
        return 1.0 / (ROPE_THETA ** (jnp.arange(0, dim, 2, dtype=F32) / dim))

    hr = MLA_ROPE // 2
    rope = (w >= MLA_NOPE) & (w < MLA_NOPE + MLA_ROPE)
    first = rope & (w < MLA_NOPE + hr)
    tab_k = _lane_table(pos, inv_freq(MLA_ROPE), np.where(rope, (w - MLA_NOPE) % hr, 0),
                        (sec == 0) & rope, np.where((sec == 1) & first, -1.0, 0.0) + ((sec == 2) & rope & ~first),
                        (sec == 0) & (w < MLA_NOPE))
    hd = DIFF_HD // 2
    first = (w % DIFF_HD) < hd
    tab_d = _lane_table(pos, inv_freq(DIFF_HD), w % hd,
                        sec == 0, np.where((sec == 1) & first, -1.0, 0.0) + ((sec == 2) & ~first), np.zeros_like(w))
    return tab_k, tab_d


def _tile(s, want):
    return min(want, s)


def kernel(x, c, positions, ada_w, ada_b, norm1_g, w_in, q_norm_g, w_uq, kv_norm_g, w_ukv, w_out_mla, diff_lambda,
           diff_subln_g, w_out_diff, conv_dw_w, conv_dw_b, conv_ln_g, conv_ln_b, w_out_conv, w_o, norm2_g, w_up,
           ffn_conv_w, ffn_conv_b, w_down, final_g):
    bsz, s, d = x.shape
    depth = ada_w.shape[0]
    tm = _tile(s, ROW_TILE)
    tf = FFN_COL_CHUNK

    tab_k, tab_d = _rope_tables(positions)
    mod = _modulation(c, ada_w, ada_b)

    o_q = 0
    o_kv = o_q + MLA_Q_RANK
    o_kr = o_kv + MLA_KV_RANK
    o_d = o_kr + MLA_ROPE
    o_c = o_d + 3 * DIFF_COLS
    o_g = o_c + 2 * CONV_CH
    w_in_b = w_in.astype(BF16)
    wq = w_in_b[:, :, o_q:o_kv]
    kr_pad = jnp.pad(w_in_b[:, :, o_kr:o_d], ((0, 0), (0, 0), (MLA_NOPE, LANES - MLA_NOPE - MLA_ROPE)))
    wkv = jnp.concatenate([w_in_b[:, :, o_kv:o_kr], kr_pad], -1)
    wd = w_in_b[:, :, o_d:o_c]
    wc = w_in_b[:, :, o_c:o_g]
    wg = w_in_b[:, :, o_g:]
    hq = MLA_NOPE + MLA_ROPE
    wuq = jnp.pad(w_uq.astype(BF16).reshape(depth, MLA_Q_RANK, MLA_HEADS, hq),
                  ((0, 0), (0, 0), (0, 0), (0, LANES - hq))).reshape(depth, MLA_Q_RANK, MLA_PAD_COLS)
    wukv_h = w_ukv.astype(BF16).reshape(depth, MLA_KV_RANK, MLA_HEADS, MLA_NOPE + MLA_V)
    wuk = jnp.pad(wukv_h[..., :MLA_NOPE], ((0, 0), (0, 0), (0, 0), (0, LANES - MLA_NOPE)))
    wukv = jnp.concatenate([wuk.reshape(depth, MLA_KV_RANK, MLA_PAD_COLS),
                            wukv_h[..., MLA_NOPE:].reshape(depth, MLA_KV_RANK, MLA_HEADS * MLA_V)], -1)
    woa, wob, woc, wo = (w.astype(BF16) for w in (w_out_mla, w_out_diff, w_out_conv, w_o))
    wup, wdn = w_up.astype(BF16), w_down.astype(BF16)
    vec = lambda a: a[:, None, :]
    g1, g2, qg, kvg, sg = vec(norm1_g), vec(norm2_g), vec(q_norm_g), vec(kv_norm_g), vec(diff_subln_g)
    cb, lng, lnb, fcb = vec(conv_dw_b), vec(conv_ln_g), vec(conv_ln_b), vec(ffn_conv_b)
    fg = final_g[None, :]

    for l in range(depth):
        lambda_init = 0.8 - 0.6 * math.exp(-0.3 * l)
        q, k, v, dq, dk, dv, u = _pre_mixer(x, mod[l], g1[l], wq[l], wkv[l], wd[l], wc[l], wuq[l], wukv[l],
                                            qg[l], kvg[l], tab_k, tab_d, tm)
        o_a = _mla_attention(q, k, v)
        o_b = _diff_attention(dq, dk, dv, diff_lambda[l], sg[l], lambda_init)
        x = _post_mixer(x, mod[l], g1[l], o_a, o_b, u, wg[l], woa[l], wob[l], woc[l], wo[l],
                        conv_dw_w[l], cb[l], lng[l], lnb[l], tm)
        x = _ffn(x, mod[l], g2[l], wup[l], ffn_conv_w[l], fcb[l], wdn[l], fg, tm, tf, l == depth - 1)
    return x
```

```python
import functools
import math

import jax
import jax.numpy as jnp
import numpy as np
from jax import lax
from jax.experimental import pallas as pl
from jax.experimental.pallas import tpu as pltpu

F32 = jnp.float32
BF16 = jnp.bfloat16

D_MODEL = 1024
MLA_HEADS = 8
MLA_NOPE = 64
MLA_ROPE = 32
MLA_V = 64
MLA_Q_RANK = 384
MLA_KV_RANK = 256
DIFF_HEADS = 4
DIFF_HD = 64
DIFF_VD = 2 * DIFF_HD
CONV_CH = 512
CONV_K = 31
D_FF = 2816
FFN_CONV_K = 3
ROPE_THETA = 10000.0
NORM_EPS = 1e-6
N_MOD = 6

LANES = 128
SUBLANES = 8
CONV_HALO = 32
VMEM_LIMIT = 56 * 1024 * 1024
NEG_BIG = -1e30
LOG2E = math.log2(math.e)
ATTN_TQ = 512
ATTN_TK = 512
ROW_TILE = 512
FFN_COL_CHUNK = 256
CONV_ROW_CHUNK = 32

DIFF_COLS = DIFF_HEADS * 2 * DIFF_HD
MLA_PAD_COLS = MLA_HEADS * LANES


def _sigmoid(x):
    return 0.5 * jnp.tanh(0.5 * x) + 0.5


def _rms(x, g):
    return x * lax.rsqrt(jnp.mean(x * x, axis=-1, keepdims=True) + NORM_EPS) * g


def _apply_rot(x, c, s_up, s_dn, half):
    return x * c + pltpu.roll(x, LANES - half, 1) * s_up + pltpu.roll(x, half, 1) * s_dn


def _expand_rope_tables(cs):
    lane = lax.broadcasted_iota(jnp.int32, cs.shape, 1)
    r32, r48, r64, r80, r96 = (pltpu.roll(cs, k, 1) for k in (32, 48, 64, 80, 96))
    zero = jnp.zeros_like(cs)
    hr = MLA_ROPE // 2
    x1 = (lane >= MLA_NOPE) & (lane < MLA_NOPE + hr)
    x2 = (lane >= MLA_NOPE + hr) & (lane < MLA_NOPE + MLA_ROPE)
    c_r = jnp.where(lane < MLA_NOPE, 1.0, jnp.where(x1, r64, jnp.where(x2, r80, zero)))
    latent = (c_r, jnp.where(x1, -r48, zero), jnp.where(x2, r64, zero))
    hd = DIFF_HD // 2
    c_d = jnp.where(lane < hd, r96, jnp.where(lane < 2 * hd, cs, jnp.where(lane < 3 * hd, r32, r64)))
    up_d = jnp.where(lane < hd, -r64, jnp.where((lane >= 2 * hd) & (lane < 3 * hd), -cs, zero))
    dn_d = jnp.where((lane >= hd) & (lane < 2 * hd), r96, jnp.where(lane >= 3 * hd, r32, zero))
    return latent, (c_d, up_d, dn_d)


def _const_spec(shape):
    nd = len(shape)
    return pl.BlockSpec(shape, lambda *_: (0,) * nd, pipeline_mode=pl.Buffered(1))


def _mod_kernel(c_ref, w_ref, b_ref, o_ref):
    c = c_ref[...]
    c_act = (c * _sigmoid(c)).astype(BF16)
    o_ref[...] = jnp.dot(c_act, w_ref[...].astype(BF16), preferred_element_type=F32) + b_ref[...]


def _modulation(c, ada_w, ada_b):
    depth, d, _ = ada_w.shape
    bsz = c.shape[0]
    out = pl.pallas_call(
        _mod_kernel,
        grid=(depth, N_MOD),
        in_specs=[
            pl.BlockSpec((bsz, d), lambda l, j: (0, 0)),
            pl.BlockSpec((None, d, d), lambda l, j: (l, 0, j)),
            pl.BlockSpec((None, 1, d), lambda l, j: (l * N_MOD + j, 0, 0)),
        ],
        out_specs=pl.BlockSpec((None, None, bsz, d), lambda l, j: (l, j, 0, 0)),
        out_shape=jax.ShapeDtypeStruct((depth, N_MOD, bsz, d), F32),
        compiler_params=pltpu.CompilerParams(dimension_semantics=("arbitrary", "arbitrary"),
                                             vmem_limit_bytes=VMEM_LIMIT),
        name="adaln_mod",
    )(c, ada_w, ada_b.reshape(depth * N_MOD, 1, d))
    return out.transpose(0, 2, 1, 3)


def _pre_kernel(x_ref, mod_ref, g1_ref, wq_ref, wkv_ref, wd_ref, wc_ref, wuq_ref, wukv_ref, qg_ref, kvg_ref,
                cs_ref, q_out, k_out, v_out, dq_out, dk_out, dv_out, u_out):
    x = x_ref[...]
    h = _rms(x, g1_ref[...]) * (1.0 + mod_ref[1:2, :]) + mod_ref[0:1, :]
    hb = h.astype(BF16)

    q_lat = jnp.dot(hb, wq_ref[...], preferred_element_type=F32)
    qn = _rms(q_lat, qg_ref[...]).astype(BF16)
    q = jnp.dot(qn, wuq_ref[...], preferred_element_type=F32)
    half = MLA_ROPE // 2
    (ck, su, sd), diff_tabs = _expand_rope_tables(cs_ref[...])
    qscale = LOG2E / math.sqrt(MLA_NOPE + MLA_ROPE)
    cq, squ, sqd = ck * qscale, su * qscale, sd * qscale
    for hd in range(MLA_HEADS):
        blk = slice(hd * LANES, (hd + 1) * LANES)
        q_out[:, blk] = _apply_rot(q[:, blk], cq, squ, sqd, half).astype(BF16)

    kv_lat = jnp.dot(hb, wkv_ref[...], preferred_element_type=F32)
    kvn = _rms(kv_lat[:, :MLA_KV_RANK], kvg_ref[...]).astype(BF16)
    kv = jnp.dot(kvn, wukv_ref[...], preferred_element_type=F32)
    k_pe = _apply_rot(kv_lat[:, MLA_KV_RANK:MLA_KV_RANK + LANES], ck, su, sd, half)
    for hd in range(MLA_HEADS):
        blk = slice(hd * LANES, (hd + 1) * LANES)
        k_out[:, blk] = (kv[:, blk] + k_pe).astype(BF16)
    v_out[...] = kv[:, MLA_PAD_COLS:].astype(BF16)

    d = jnp.dot(hb, wd_ref[...], preferred_element_type=F32)
    half = DIFF_HD // 2
    cd, su, sd = diff_tabs
    scale = LOG2E / math.sqrt(DIFF_HD)
    for hd in range(DIFF_HEADS):
        blk = slice(hd * LANES, (hd + 1) * LANES)
        kblk = slice(DIFF_COLS + hd * LANES, DIFF_COLS + (hd + 1) * LANES)
        dq_out[:, blk] = (_apply_rot(d[:, blk], cd, su, sd, half) * scale).astype(BF16)
        dk_out[:, blk] = _apply_rot(d[:, kblk], cd, su, sd, half).astype(BF16)
    dv_out[...] = d[:, 2 * DIFF_COLS:].astype(BF16)

    cv = jnp.dot(hb, wc_ref[...], preferred_element_type=F32)
    u_out[...] = cv[:, :CONV_CH] * _sigmoid(cv[:, CONV_CH:])


def _pre_mixer(x, mod_l, g1, wq, wkv, wd, wc, wuq, wukv, qg, kvg, rope_cs, tm):
    bsz, s, d = x.shape
    row = lambda w: pl.BlockSpec((None, tm, w), lambda b, i: (b, i, 0))
    outs = [(MLA_PAD_COLS, BF16), (MLA_PAD_COLS, BF16), (MLA_HEADS * MLA_V, BF16),
            (DIFF_COLS, BF16), (DIFF_COLS, BF16), (DIFF_COLS, BF16), (CONV_CH, F32)]
    return pl.pallas_call(
        _pre_kernel,
        grid=(bsz, s // tm),
        in_specs=[row(d), pl.BlockSpec((None, N_MOD, d), lambda b, i: (b, 0, 0)), _const_spec(g1.shape),
                  _const_spec(wq.shape), _const_spec(wkv.shape), _const_spec(wd.shape), _const_spec(wc.shape),
                  _const_spec(wuq.shape), _const_spec(wukv.shape), _const_spec(qg.shape), _const_spec(kvg.shape),
                  row(LANES)],
        out_specs=[row(w) for w, _ in outs],
        out_shape=[jax.ShapeDtypeStruct((bsz, s, w), dt) for w, dt in outs],
        compiler_params=pltpu.CompilerParams(dimension_semantics=("parallel", "parallel"),
                                             vmem_limit_bytes=VMEM_LIMIT),
        name="pre_mixer",
    )(x, mod_l, g1, wq, wkv, wd, wc, wuq, wukv, qg, kvg, rope_cs)


def _flash_stacked(q2t, k_ref, vt_ref, scratch, row0, tq, tk):
    s_buf, m_ref, l_ref, acc_ref = scratch
    rows2 = 2 * tq
    n_full = row0 // tk

    def scores(kb, slot):
        ks = pl.multiple_of(kb * tk, tk)
        s_buf[slot] = jnp.dot(k_ref[pl.ds(ks, tk), :], q2t, preferred_element_type=F32)

    def update(kb, slot, masked):
        s = s_buf[slot]
        if masked:
            c = lax.broadcasted_iota(jnp.int32, (tk, rows2), 1)
            pos = row0 + jnp.where(c >= tq, c - tq, c)
            keys = kb * tk + lax.broadcasted_iota(jnp.int32, (tk, rows2), 0)
            s = jnp.where(keys <= pos, s, NEG_BIG)
        m = m_ref[...]
        m_new = jnp.maximum(m, jnp.max(s, axis=0, keepdims=True))
        alpha = jnp.exp2(m - m_new)
        p = jnp.exp2(s - m_new)
        l_ref[...] = alpha * l_ref[...] + jnp.sum(p, axis=0, keepdims=True)
        m_ref[...] = m_new
        pv = jnp.dot(vt_ref[kb], p.astype(BF16), preferred_element_type=F32)
        acc_ref[...] = alpha * acc_ref[...] + pv

    m_ref[...] = jnp.full(m_ref.shape, NEG_BIG, F32)
    l_ref[...] = jnp.zeros(l_ref.shape, F32)
    acc_ref[...] = jnp.zeros(acc_ref.shape, F32)
    scores(0, 0)

    def pair(jj, carry):
        kb = 2 * jj
        scores(kb + 1, 1)
        update(kb, 0, False)
        scores(kb + 2, 0)
        update(kb + 1, 1, False)
        return carry

    lax.fori_loop(0, n_full // 2, pair, 0)
    odd = n_full % 2 == 1

    @pl.when(odd)
    def _():
        scores(n_full, 1)
        update(n_full - 1, 0, False)
        update(n_full, 1, True)

    @pl.when(jnp.logical_not(odd))
    def _():
        update(n_full, 0, True)

    return acc_ref[...] / l_ref[...]


def _flash_scratch(s, tq, tk):
    rows2 = 2 * tq
    return [pltpu.VMEM((s // tk, LANES, tk), BF16), pltpu.VMEM((2, tk, rows2), F32), pltpu.VMEM((1, rows2), F32),
            pltpu.VMEM((1, rows2), F32), pltpu.VMEM((LANES, rows2), F32)]


def _transpose_values(v_ref, vt_ref, tk):
    @pl.when(pl.program_id(2) == 0)
    def _():
        for kb in range(vt_ref.shape[0]):
            vt_ref[kb] = v_ref[kb * tk:(kb + 1) * tk, :].astype(F32).T.astype(BF16)


def _mla_attn_kernel(q_ref, k_ref, v_ref, o_ref, vt_ref, *scratch, tq, tk):
    row0 = pl.program_id(2) * tq
    _transpose_values(v_ref, vt_ref, tk)
    qt = q_ref[...].astype(F32).T.astype(BF16)
    zero = jnp.zeros((LANES, tq), BF16)
    q2t = jnp.concatenate([jnp.concatenate([qt[:LANES], zero], axis=1),
                           jnp.concatenate([zero, qt[LANES:]], axis=1)], axis=0)
    ot = _flash_stacked(q2t, k_ref, vt_ref, scratch, row0, tq, tk)
    feat = lax.broadcasted_iota(jnp.int32, (LANES, tq), 0)
    o_ref[...] = jnp.where(feat < MLA_V, ot[:, :tq], ot[:, tq:]).T.astype(BF16)


def _mla_attention(q, k, v):
    bsz, s, _ = q.shape
    tq, tk = _tile(s, ATTN_TQ), _tile(s, ATTN_TK)
    return pl.pallas_call(
        functools.partial(_mla_attn_kernel, tq=tq, tk=tk),
        grid=(bsz, MLA_HEADS // 2, s // tq),
        in_specs=[pl.BlockSpec((None, tq, 2 * LANES), lambda b, j, i: (b, i, j)),
                  pl.BlockSpec((None, s, 2 * LANES), lambda b, j, i: (b, 0, j)),
                  pl.BlockSpec((None, s, LANES), lambda b, j, i: (b, 0, j))],
        out_specs=pl.BlockSpec((None, tq, LANES), lambda b, j, i: (b, i, j)),
        out_shape=jax.ShapeDtypeStruct((bsz, s, MLA_HEADS * MLA_V), BF16),
        scratch_shapes=_flash_scratch(s, tq, tk),
        compiler_params=pltpu.CompilerParams(dimension_semantics=("arbitrary", "arbitrary", "arbitrary"),
                                             vmem_limit_bytes=VMEM_LIMIT),
        name="mla_attention",
    )(q, k, v)


def _diff_attn_kernel(q_ref, k_ref, v_ref, dl_ref, g_ref, o_ref, vt_ref, *scratch, tq, tk, lambda_init):
    row0 = pl.program_id(2) * tq
    _transpose_values(v_ref, vt_ref, tk)
    qt = q_ref[...].astype(F32).T.astype(BF16)
    feat = lax.broadcasted_iota(jnp.int32, qt.shape, 0)
    zero = jnp.zeros_like(qt)
    q2t = jnp.concatenate([jnp.where(feat < DIFF_HD, qt, zero), jnp.where(feat < DIFF_HD, zero, qt)], axis=1)
    ot = _flash_stacked(q2t, k_ref, vt_ref, scratch, row0, tq, tk)
    dl = dl_ref[...]
    lam = (jnp.exp(jnp.sum(dl[0:1, :] * dl[1:2, :], keepdims=True))
           - jnp.exp(jnp.sum(dl[2:3, :] * dl[3:4, :], keepdims=True)) + lambda_init)
    o = (ot[:, :tq] - lam * ot[:, tq:]).T
    o_ref[...] = (_rms(o, g_ref[...]) * (1.0 - lambda_init)).astype(BF16)


def _diff_attention(q, k, v, dl, g, lambda_init):
    bsz, s, _ = q.shape
    tq, tk = _tile(s, ATTN_TQ), _tile(s, ATTN_TK)
    return pl.pallas_call(
        functools.partial(_diff_attn_kernel, tq=tq, tk=tk, lambda_init=lambda_init),
        grid=(bsz, DIFF_HEADS, s // tq),
        in_specs=[pl.BlockSpec((None, tq, LANES), lambda b, j, i: (b, i, j)),
                  pl.BlockSpec((None, s, LANES), lambda b, j, i: (b, 0, j)),
                  pl.BlockSpec((None, s, LANES), lambda b, j, i: (b, 0, j)),
                  pl.BlockSpec(dl.shape, lambda b, j, i: (0, 0)),
                  pl.BlockSpec(g.shape, lambda b, j, i: (0, 0))],
        out_specs=pl.BlockSpec((None, tq, LANES), lambda b, j, i: (b, i, j)),
        out_shape=jax.ShapeDtypeStruct((bsz, s, DIFF_HEADS * DIFF_VD), BF16),
        scratch_shapes=_flash_scratch(s, tq, tk),
        compiler_params=pltpu.CompilerParams(dimension_semantics=("arbitrary", "arbitrary", "arbitrary"),
                                             vmem_limit_bytes=VMEM_LIMIT),
        name="diff_attention",
    )(q, k, v, dl, g)


def _post_kernel(x_ref, mod_ref, g1_ref, oa_ref, ob_ref, u_ref, wg_ref, woa_ref, wob_ref, woc_ref, wo_ref,
                 cw_ref, cb_ref, lng_ref, lnb_ref, out_ref, ubuf, cbuf, *, tm, chunk):
    i = pl.program_id(1)

    @pl.when(i == 0)
    def _():
        ubuf[0, 0:CONV_HALO, :] = jnp.zeros((CONV_HALO, CONV_CH), F32)

    @pl.when(i > 0)
    def _():
        ubuf[0, 0:CONV_HALO, :] = ubuf[0, tm:tm + CONV_HALO, :]

    rows = tm + CONV_HALO
    ubuf[0, CONV_HALO:rows, :] = u_ref[...]
    for s in range(1, SUBLANES):
        ubuf[s, SUBLANES:rows, :] = ubuf[0, SUBLANES - s:rows - s, :]

    def conv_chunks(lo, hi):
        for c in range(lo, hi):
            acc = jnp.broadcast_to(cb_ref[...], (chunk, CONV_CH))
            for j in range(CONV_K):
                whole, s = divmod(CONV_K - 1 - j, SUBLANES)
                r = c * chunk + CONV_HALO - whole * SUBLANES
                acc = acc + cw_ref[j:j + 1, :] * ubuf[s, r:r + chunk, :]
            cbuf[c * chunk:(c + 1) * chunk, :] = acc

    x = x_ref[...]
    h = _rms(x, g1_ref[...]) * (1.0 + mod_ref[1:2, :]) + mod_ref[0:1, :]
    hb = h.astype(BF16)
    d = x.shape[-1]
    n_chunks = tm // chunk
    quarter = -(-n_chunks // 4)
    y_a = jnp.dot(oa_ref[...], woa_ref[...], preferred_element_type=F32)
    conv_chunks(0, min(quarter, n_chunks))
    merged = _sigmoid(jnp.dot(hb, wg_ref[:, 0:d], preferred_element_type=F32)) * y_a
    conv_chunks(min(quarter, n_chunks), min(2 * quarter, n_chunks))
    y_b = jnp.dot(ob_ref[...], wob_ref[...], preferred_element_type=F32)
    conv_chunks(min(2 * quarter, n_chunks), min(3 * quarter, n_chunks))
    merged = merged + _sigmoid(jnp.dot(hb, wg_ref[:, d:2 * d], preferred_element_type=F32)) * y_b
    conv_chunks(min(3 * quarter, n_chunks), n_chunks)
    gate_c = _sigmoid(jnp.dot(hb, wg_ref[:, 2 * d:3 * d], preferred_element_type=F32))

    uc = cbuf[...]
    mu = jnp.mean(uc, axis=-1, keepdims=True)
    xc = uc - mu
    var = jnp.mean(xc * xc, axis=-1, keepdims=True)
    ln = xc * lax.rsqrt(var + NORM_EPS) * lng_ref[...] + lnb_ref[...]
    y_c = jnp.dot((ln * _sigmoid(ln)).astype(BF16), woc_ref[...], preferred_element_type=F32)
    merged = merged + gate_c * y_c
    out_ref[...] = x + mod_ref[2:3, :] * jnp.dot(merged.astype(BF16), wo_ref[...], preferred_element_type=F32)


def _post_mixer(x, mod_l, g1, o_a, o_b, u, wg, woa, wob, woc, wo, cw, cb, lng, lnb, tm):
    bsz, s, d = x.shape
    row = lambda w: pl.BlockSpec((None, tm, w), lambda b, i: (b, i, 0))
    consts = (g1,)
    weights = (wg, woa, wob, woc, wo, cw, cb, lng, lnb)
    return pl.pallas_call(
        functools.partial(_post_kernel, tm=tm, chunk=min(CONV_ROW_CHUNK, tm)),
        grid=(bsz, s // tm),
        in_specs=[row(d), pl.BlockSpec((None, N_MOD, d), lambda b, i: (b, 0, 0))]
                 + [_const_spec(a.shape) for a in consts]
                 + [row(o_a.shape[-1]), row(o_b.shape[-1]), row(CONV_CH)]
                 + [_const_spec(a.shape) for a in weights],
        out_specs=row(d),
        out_shape=jax.ShapeDtypeStruct((bsz, s, d), F32),
        scratch_shapes=[pltpu.VMEM((SUBLANES, tm + CONV_HALO, CONV_CH), F32), pltpu.VMEM((tm, CONV_CH), F32)],
        compiler_params=pltpu.CompilerParams(dimension_semantics=("arbitrary", "arbitrary"),
                                             vmem_limit_bytes=VMEM_LIMIT),
        name="post_mixer",
    )(x, mod_l, g1, o_a, o_b, u, *weights)


def _shift_rows(u, prev, k):
    rolled = pltpu.roll(u, k, 0)
    first = lax.broadcasted_iota(jnp.int32, prev.shape, 0) < k
    head = jnp.where(first, pltpu.roll(prev, k, 0), rolled[:SUBLANES])
    return jnp.concatenate([head, rolled[SUBLANES:]], axis=0)


def _ffn_kernel(x_ref, mod_ref, g2_ref, wup_ref, fcw_ref, fcb_ref, wdn_ref, fg_ref, out_ref, carry, gbuf,
                *, tm, tf, final):
    i = pl.program_id(1)

    @pl.when(i == 0)
    def _():
        carry[...] = jnp.zeros(carry.shape, F32)

    x = x_ref[...]
    h = _rms(x, g2_ref[...]) * (1.0 + mod_ref[4:5, :]) + mod_ref[3:4, :]
    hb = h.astype(BF16)
    n_chunks = D_FF // tf

    def cols(part, c):
        return slice(part * D_FF + c * tf, part * D_FF + (c + 1) * tf)

    def up_proj(c):
        return [jnp.dot(hb, wup_ref[:, cols(part, c)], preferred_element_type=F32) for part in range(2)]

    ups = up_proj(0)
    for c in range(n_chunks):
        ups_next = up_proj(c + 1) if c + 1 < n_chunks else None
        halves = []
        for part in range(2):
            cs = cols(part, c)
            u = ups[part]
            prev = carry[:, cs]
            carry[:, cs] = u[tm - SUBLANES:tm, :]
            y = fcb_ref[:, cs] + fcw_ref[2:3, cs] * u
            y = y + fcw_ref[1:2, cs] * _shift_rows(u, prev, 1)
            y = y + fcw_ref[0:1, cs] * _shift_rows(u, prev, 2)
            halves.append(y)
        a, b = halves
        gbuf[:, c * tf:(c + 1) * tf] = (a * _sigmoid(a) * b).astype(BF16)
        ups = ups_next
    y = x + mod_ref[5:6, :] * jnp.dot(gbuf[...], wdn_ref[...], preferred_element_type=F32)
    if final:
        y = _rms(y, fg_ref[...])
    out_ref[...] = y


def _ffn(x, mod_l, g2, wup, fcw, fcb, wdn, fg, tm, tf, final):
    bsz, s, d = x.shape
    row = pl.BlockSpec((None, tm, d), lambda b, i: (b, i, 0))
    consts = (g2, wup, fcw, fcb, wdn, fg)
    return pl.pallas_call(
        functools.partial(_ffn_kernel, tm=tm, tf=tf, final=final),
        grid=(bsz, s // tm),
        in_specs=[row, pl.BlockSpec((None, N_MOD, d), lambda b, i: (b, 0, 0))]
                 + [_const_spec(a.shape) for a in consts],
        out_specs=row,
        out_shape=jax.ShapeDtypeStruct((bsz, s, d), F32),
        scratch_shapes=[pltpu.VMEM((SUBLANES, 2 * D_FF), F32), pltpu.VMEM((tm, D_FF), BF16)],
        compiler_params=pltpu.CompilerParams(dimension_semantics=("arbitrary", "arbitrary"),
                                             vmem_limit_bytes=VMEM_LIMIT),
        name="conv_gated_mlp",
    )(x, mod_l, *consts)


def _rope_cos_sin(positions):
    pos = positions.astype(F32)[..., None]
    hr, hd = MLA_ROPE // 2, DIFF_HD // 2

    def inv_freq(dim):
        return 1.0 / (ROPE_THETA ** (jnp.arange(0, dim, 2, dtype=F32) / dim))

    inv = jnp.concatenate([inv_freq(MLA_ROPE), inv_freq(MLA_ROPE), inv_freq(DIFF_HD), inv_freq(DIFF_HD),
                           jnp.zeros((LANES - 2 * hr - 2 * hd,), F32)])
    lane = np.arange(LANES)
    is_cos = (lane < hr) | ((lane >= 2 * hr) & (lane < 2 * hr + hd))
    is_sin = ((lane >= hr) & (lane < 2 * hr)) | ((lane >= 2 * hr + hd) & (lane < 2 * hr + 2 * hd))
    ang = pos * inv
    return jnp.asarray(is_cos, F32) * jnp.cos(ang) + jnp.asarray(is_sin, F32) * jnp.sin(ang)


def _tile(s, want):
    return min(want, s)


def kernel(x, c, positions, ada_w, ada_b, norm1_g, w_in, q_norm_g, w_uq, kv_norm_g, w_ukv, w_out_mla, diff_lambda,
           diff_subln_g, w_out_diff, conv_dw_w, conv_dw_b, conv_ln_g, conv_ln_b, w_out_conv, w_o, norm2_g, w_up,
           ffn_conv_w, ffn_conv_b, w_down, final_g):
    bsz, s, d = x.shape
    depth = ada_w.shape[0]
    tm = _tile(s, ROW_TILE)
    tf = FFN_COL_CHUNK

    rope_cs = _rope_cos_sin(positions)
    mod = _modulation(c, ada_w, ada_b)

    o_q = 0
    o_kv = o_q + MLA_Q_RANK
    o_kr = o_kv + MLA_KV_RANK
    o_d = o_kr + MLA_ROPE
    o_c = o_d + 3 * DIFF_COLS
    o_g = o_c + 2 * CONV_CH
    w_in_b = w_in.astype(BF16)
    wq = w_in_b[:, :, o_q:o_kv]
    kr_pad = jnp.pad(w_in_b[:, :, o_kr:o_d], ((0, 0), (0, 0), (MLA_NOPE, LANES - MLA_NOPE - MLA_ROPE)))
    wkv = jnp.concatenate([w_in_b[:, :, o_kv:o_kr], kr_pad], -1)
    wd = w_in_b[:, :, o_d:o_c]
    wc = w_in_b[:, :, o_c:o_g]
    wg = w_in_b[:, :, o_g:]
    hq = MLA_NOPE + MLA_ROPE
    wuq = jnp.pad(w_uq.astype(BF16).reshape(depth, MLA_Q_RANK, MLA_HEADS, hq),
                  ((0, 0), (0, 0), (0, 0), (0, LANES - hq))).reshape(depth, MLA_Q_RANK, MLA_PAD_COLS)
    wukv_h = w_ukv.astype(BF16).reshape(depth, MLA_KV_RANK, MLA_HEADS, MLA_NOPE + MLA_V)
    wuk = jnp.pad(wukv_h[..., :MLA_NOPE], ((0, 0), (0, 0), (0, 0), (0, LANES - MLA_NOPE)))
    wukv = jnp.concatenate([wuk.reshape(depth, MLA_KV_RANK, MLA_PAD_COLS),
                            wukv_h[..., MLA_NOPE:].reshape(depth, MLA_KV_RANK, MLA_HEADS * MLA_V)], -1)
    woa, wob, woc, wo = (w.astype(BF16) for w in (w_out_mla, w_out_diff, w_out_conv, w_o))
    wup, wdn = w_up.astype(BF16), w_down.astype(BF16)
    vec = lambda a: a[:, None, :]
    g1, g2, qg, kvg, sg = vec(norm1_g), vec(norm2_g), vec(q_norm_g), vec(kv_norm_g), vec(diff_subln_g)
    cb, lng, lnb, fcb = vec(conv_dw_b), vec(conv_ln_g), vec(conv_ln_b), vec(ffn_conv_b)
    fg = final_g[None, :]

    for l in range(depth):
        lambda_init = 0.8 - 0.6 * math.exp(-0.3 * l)
        q, k, v, dq, dk, dv, u = _pre_mixer(x, mod[l], g1[l], wq[l], wkv[l], wd[l], wc[l], wuq[l], wukv[l],
                                            qg[l], kvg[l], rope_cs, tm)
        o_a = _mla_attention(q, k, v)
        o_b = _diff_attention(dq, dk, dv, diff_lambda[l], sg[l], lambda_init)
        x = _post_mixer(x, mod[l], g1[l], o_a, o_b, u, wg[l], woa[l], wob[l], woc[l], wo[l],
                        conv_dw_w[l], cb[l], lng[l], lnb[l], tm)
        x = _ffn(x, mod[l], g2[l], wup[l], ffn_conv_w[l], fcb[l], wdn[l], fg, tm, tf, l == depth - 1)
    return x
```
